```python
import jax, jax.numpy as jnp
from jax import lax
import numpy as np

D_MODEL = 1024
BATCH = 4
SEQ = 8192
DEPTH = 2

HEAD_DIM = 64
ATTN_HEADS = (D_MODEL // 2) // HEAD_DIM
ATTN_KV_HEADS = ATTN_HEADS // 4
WINDOW = 128
ATTN_BLOCK = 128
ROPE_THETA = 10000.0
CONV_CH = D_MODEL // 2
CONV_WIDTH = 3
Q_W = ATTN_HEADS * HEAD_DIM
KV_W = ATTN_KV_HEADS * HEAD_DIM
EVEN_IN_W = Q_W + 2 * KV_W + 3 * CONV_CH
EVEN_MIX_W = Q_W + CONV_CH
DN_HEAD_DIM = 128
DN_HEADS = D_MODEL // DN_HEAD_DIM
DN_W = DN_HEADS * DN_HEAD_DIM
DN_CONV_WIDTH = 4
DN_CHUNK = 64
ODD_IN_W = 4 * DN_W + 2 * DN_HEADS
D_FF = -(-8 * D_MODEL // (3 * 256)) * 256
EPS = 1e-6
N_EVEN = (DEPTH + 1) // 2
N_ODD = DEPTH // 2

kernel_name = 'hybrid_swa_shortconv_gdn_block'


def rms_norm(x, g):
    xf = x.astype(jnp.float32)
    y = xf * lax.rsqrt(jnp.mean(xf * xf, axis=-1, keepdims=True) + EPS) * g.astype(jnp.float32)
    return y.astype(x.dtype)


def l2_norm(x):
    return x * lax.rsqrt(jnp.sum(x * x, axis=-1, keepdims=True) + EPS)


def rope_tables(seq_len):
    inv_freq = ROPE_THETA ** (-jnp.arange(0, HEAD_DIM, 2, dtype=jnp.float32) / HEAD_DIM)
    ang = jnp.arange(seq_len, dtype=jnp.float32)[:, None] * inv_freq[None, :]
    return jnp.cos(ang), jnp.sin(ang)


def apply_rope(x, cos, sin):
    xf = x.astype(jnp.float32)
    half = HEAD_DIM // 2
    x1, x2 = xf[..., :half], xf[..., half:]
    c, s = cos[None, :, None, :], sin[None, :, None, :]
    return jnp.concatenate([x1 * c - x2 * s, x2 * c + x1 * s], axis=-1).astype(x.dtype)


def causal_depthwise_conv(u, w):
    width, ch = w.shape
    return lax.conv_general_dilated(
        u, w[:, None, :].astype(u.dtype), window_strides=(1,), padding=[(width - 1, 0)],
        dimension_numbers=('NWC', 'WIO', 'NWC'), feature_group_count=ch)


def sliding_window_attention(q, k, v, sinks):
    bsz, t, h, d = q.shape
    kvh = k.shape[2]
    grp = h // kvh
    nb = t // ATTN_BLOCK
    qb = q.reshape(bsz, nb, ATTN_BLOCK, kvh, grp, d)

    def band(z):
        pad = jnp.zeros((bsz, ATTN_BLOCK, kvh, d), z.dtype)
        zp = jnp.concatenate([pad, z], axis=1).reshape(bsz, nb + 1, ATTN_BLOCK, kvh, d)
        return jnp.concatenate([zp[:, :-1], zp[:, 1:]], axis=2)

    kb, vb = band(k), band(v)
    s = jnp.einsum('bnqkgd,bnskd->bnkgqs', qb, kb,
                   preferred_element_type=jnp.float32) * (d ** -0.5)
    blk = jnp.arange(nb)[:, None] * ATTN_BLOCK
    q_pos = blk + jnp.arange(ATTN_BLOCK)[None, :]
    k_pos = blk - ATTN_BLOCK + jnp.arange(2 * ATTN_BLOCK)[None, :]
    diff = q_pos[:, :, None] - k_pos[:, None, :]
    valid = (diff >= 0) & (diff < WINDOW) & (k_pos[:, None, :] >= 0)
    s = jnp.where(valid[None, :, None, None], s, -jnp.inf)
    sink = jnp.broadcast_to(sinks.astype(jnp.float32).reshape(kvh, grp)[None, None, :, :, None, None],
                            s.shape[:-1] + (1,))
    p = jax.nn.softmax(jnp.concatenate([s, sink], axis=-1), axis=-1)[..., :-1]
    o = jnp.einsum('bnkgqs,bnskd->bnqkgd', p.astype(v.dtype), vb)
    return o.reshape(bsz, t, h * d)


def attn_conv_mixer(h, w_in, q_gain, k_gain, sinks, conv_w, w_out, cos, sin):
    bsz, t, _ = h.shape
    proj = h @ w_in
    q, k, v, gate_b, gate_c, xin = jnp.split(
        proj, np.cumsum([Q_W, KV_W, KV_W, CONV_CH, CONV_CH]).tolist(), axis=-1)
    q = apply_rope(rms_norm(q.reshape(bsz, t, ATTN_HEADS, HEAD_DIM), q_gain), cos, sin)
    k = apply_rope(rms_norm(k.reshape(bsz, t, ATTN_KV_HEADS, HEAD_DIM), k_gain), cos, sin)
    v = v.reshape(bsz, t, ATTN_KV_HEADS, HEAD_DIM)
    y_attn = sliding_window_attention(q, k, v, sinks)
    y_conv = gate_b * causal_depthwise_conv(gate_c * xin, conv_w)
    return jnp.concatenate([y_attn, y_conv], axis=-1) @ w_out


def chunk_gated_delta_rule(q, k, v, beta, g):
    bsz, t, h, dk = q.shape
    dv = v.shape[-1]
    n = t // DN_CHUNK

    def chunks(z):
        return z.reshape(bsz, n, DN_CHUNK, h, -1).transpose(0, 1, 3, 2, 4)

    q, k, v = chunks(q), chunks(k), chunks(v)
    beta = chunks(beta[..., None])[..., 0]
    gc = jnp.cumsum(chunks(g[..., None])[..., 0], axis=-1)
    idx = jnp.arange(DN_CHUNK)
    lower = idx[:, None] >= idx[None, :]
    strict = idx[:, None] > idx[None, :]
    gamma = jnp.exp(jnp.where(lower, gc[..., :, None] - gc[..., None, :], -jnp.inf))
    kb = k * beta[..., None]
    a_mat = jnp.where(strict, jnp.einsum('bnhcd,bnhsd->bnhcs', kb, k) * gamma, 0.0)
    m_mat = a_mat + jnp.eye(DN_CHUNK, dtype=a_mat.dtype)
    rhs = jnp.concatenate([v * beta[..., None], kb * jnp.exp(gc)[..., None]], axis=-1)
    sol = lax.linalg.triangular_solve(m_mat, rhs, left_side=True, lower=True, unit_diagonal=True)
    u, w = sol[..., :dv], sol[..., dv:]
    attn_intra = jnp.einsum('bnhcd,bnhsd->bnhcs', q, k) * gamma
    q_dec = q * jnp.exp(gc)[..., None]
    k_dec = k * jnp.exp(gc[..., -1:] - gc)[..., None]
    chunk_decay = jnp.exp(gc[..., -1])

    def step(state, xs):
        qd, kd, u_i, w_i, a_i, dec = xs
        v_new = u_i - jnp.einsum('bhcd,bhde->bhce', w_i, state)
        o = jnp.einsum('bhcd,bhde->bhce', qd, state) + jnp.einsum('bhcs,bhse->bhce', a_i, v_new)
        state = state * dec[..., None, None] + jnp.einsum('bhcd,bhce->bhde', kd, v_new)
        return state, o

    xs = tuple(jnp.moveaxis(z, 1, 0) for z in (q_dec, k_dec, u, w, attn_intra, chunk_decay))
    s0 = jnp.zeros((bsz, h, dk, dv), jnp.float32)
    _, o = lax.scan(step, s0, xs)
    return o.transpose(1, 0, 3, 2, 4).reshape(bsz, t, h, dv)


def gated_deltanet_mixer(h, w_in, conv_w, a_log, dt_bias, o_gain, w_out):
    bsz, t, _ = h.shape
    proj = h @ w_in
    qkv, z, b_logit, a_logit = jnp.split(
        proj, [3 * DN_W, 4 * DN_W, 4 * DN_W + DN_HEADS], axis=-1)
    qkv = jax.nn.silu(causal_depthwise_conv(qkv, conv_w)).astype(jnp.float32)
    q, k, v = jnp.split(qkv, 3, axis=-1)
    q = l2_norm(q.reshape(bsz, t, DN_HEADS, DN_HEAD_DIM)) * (DN_HEAD_DIM ** -0.5)
    k = l2_norm(k.reshape(bsz, t, DN_HEADS, DN_HEAD_DIM))
    v = v.reshape(bsz, t, DN_HEADS, DN_HEAD_DIM)
    beta = jax.nn.sigmoid(b_logit.astype(jnp.float32))
    g = -jnp.exp(a_log.astype(jnp.float32)) * jax.nn.softplus(
        a_logit.astype(jnp.float32) + dt_bias.astype(jnp.float32))
    o = chunk_gated_delta_rule(q, k, v, beta, g)
    zf = z.reshape(bsz, t, DN_HEADS, DN_HEAD_DIM).astype(jnp.float32)
    o = rms_norm(o, o_gain) * jax.nn.silu(zf)
    return o.reshape(bsz, t, DN_W).astype(h.dtype) @ w_out


def swiglu_ffn(h, w_gate_up, w_down):
    gate, up = jnp.split(h @ w_gate_up, 2, axis=-1)
    return (jax.nn.silu(gate) * up) @ w_down


def setup_inputs(seed: int = 0) -> dict:
    key = jax.random.key(seed)
    ks = jax.random.split(key, 20)
    nrm = lambda k_, shape, scale: jax.random.normal(k_, shape, jnp.float32) * scale
    dt = jnp.exp(jax.random.uniform(ks[12], (N_ODD, DN_HEADS), jnp.float32,
                                    np.log(1e-3), np.log(1e-1)))
    return {
        'x': nrm(ks[0], (BATCH, SEQ, D_MODEL), 1.0),
        'even_norm': 1.0 + nrm(ks[1], (N_EVEN, D_MODEL), 0.02),
        'even_w_in': nrm(ks[2], (N_EVEN, D_MODEL, EVEN_IN_W), D_MODEL ** -0.5),
        'even_q_gain': 1.0 + nrm(ks[3], (N_EVEN, HEAD_DIM), 0.02),
        'even_k_gain': 1.0 + nrm(ks[4], (N_EVEN, HEAD_DIM), 0.02),
        'even_sinks': nrm(ks[5], (N_EVEN, ATTN_HEADS), 0.5),
        'even_conv_w': nrm(ks[6], (N_EVEN, CONV_WIDTH, CONV_CH), CONV_WIDTH ** -0.5),
        'even_w_out': nrm(ks[7], (N_EVEN, EVEN_MIX_W, D_MODEL), EVEN_MIX_W ** -0.5),
        'odd_norm': 1.0 + nrm(ks[8], (N_ODD, D_MODEL), 0.02),
        'odd_w_in': nrm(ks[9], (N_ODD, D_MODEL, ODD_IN_W), D_MODEL ** -0.5),
        'odd_conv_w': nrm(ks[10], (N_ODD, DN_CONV_WIDTH, 3 * DN_W), DN_CONV_WIDTH ** -0.5),
        'odd_a_log': jnp.log(jax.random.uniform(ks[11], (N_ODD, DN_HEADS), jnp.float32, 1.0, 16.0)),
        'odd_dt_bias': dt + jnp.log(-jnp.expm1(-dt)),
        'odd_o_gain': 1.0 + nrm(ks[13], (N_ODD, DN_HEAD_DIM), 0.02),
        'odd_w_out': nrm(ks[14], (N_ODD, DN_W, D_MODEL), DN_W ** -0.5),
        'ffn_norm': 1.0 + nrm(ks[15], (DEPTH, D_MODEL), 0.02),
        'ffn_w_gate_up': nrm(ks[16], (DEPTH, D_MODEL, 2 * D_FF), D_MODEL ** -0.5),
        'ffn_w_down': nrm(ks[17], (DEPTH, D_FF, D_MODEL), D_FF ** -0.5),
    }


def reference(x, even_norm, even_w_in, even_q_gain, even_k_gain, even_sinks, even_conv_w,
              even_w_out, odd_norm, odd_w_in, odd_conv_w, odd_a_log, odd_dt_bias, odd_o_gain,
              odd_w_out, ffn_norm, ffn_w_gate_up, ffn_w_down):
    cos, sin = rope_tables(x.shape[1])
    for layer in range(DEPTH):
        i = layer // 2
        if layer % 2 == 0:
            x = x + attn_conv_mixer(rms_norm(x, even_norm[i]), even_w_in[i], even_q_gain[i],
                                    even_k_gain[i], even_sinks[i], even_conv_w[i],
                                    even_w_out[i], cos, sin)
        else:
            x = x + gated_deltanet_mixer(rms_norm(x, odd_norm[i]), odd_w_in[i], odd_conv_w[i],
                                         odd_a_log[i], odd_dt_bias[i], odd_o_gain[i],
                                         odd_w_out[i])
        x = x + swiglu_ffn(rms_norm(x, ffn_norm[layer]), ffn_w_gate_up[layer], ffn_w_down[layer])
    return x
```

```python
import functools

import jax
import jax.numpy as jnp
from jax import lax
from jax.experimental import pallas as pl
from jax.experimental.pallas import tpu as pltpu

D_MODEL = 1024
D_FF = 2816
EPS = 1e-6

HEAD_DIM = 64
ATTN_HEADS = 8
ATTN_KV_HEADS = 2
ATTN_BLOCK = 128
ROPE_THETA = 10000.0
CONV_CH = 512
Q_W = ATTN_HEADS * HEAD_DIM
KV_W = ATTN_KV_HEADS * HEAD_DIM
DN_HEAD_DIM = 128
DN_HEADS = 8
DN_W = DN_HEADS * DN_HEAD_DIM
DN_CONV_WIDTH = 4
DN_CHUNK = 64

LANES = 128
SUBLANES = 8
MXU_DIM = 256
VMEM_LIMIT_BYTES = 56 * 1024 * 1024

ROW_TILE = 512
FFN_CHUNK = 1408
MIX_ROWS = 512
NEG_BIG = -1e30

BF16 = jnp.bfloat16
F32 = jnp.float32


def _rms_norm_rows(xf, gain):
    ms = jnp.mean(xf * xf, axis=-1, keepdims=True)
    return xf * lax.rsqrt(ms + EPS) * gain


def _sigmoid(v):
    return 1.0 / (1.0 + jnp.exp(-v))


def _silu(v):
    return v * _sigmoid(v)


def _dot(a, b):
    return jnp.dot(a, b, preferred_element_type=F32)


def _dot_nt(a, b):
    return lax.dot_general(a, b, (((1,), (1,)), ((), ())), preferred_element_type=F32)


def _dot_tn(a, b):
    return lax.dot_general(a, b, (((0,), (0,)), ((), ())), preferred_element_type=F32)


def _const_spec(shape):
    nd = len(shape)
    return pl.BlockSpec(shape, lambda *_: (0,) * nd, pipeline_mode=pl.Buffered(1))


def _block_ones(n, block):
    r = lax.broadcasted_iota(jnp.int32, (n, n), 0) // block
    c = lax.broadcasted_iota(jnp.int32, (n, n), 1) // block
    return (r == c).astype(BF16)


def _group_sumsq(v, block):
    ones = _block_ones(MXU_DIM, block)
    sq = (v * v).astype(BF16)
    parts = [_dot(sq[:, c:c + MXU_DIM], ones) for c in range(0, v.shape[1], MXU_DIM)]
    return parts[0] if len(parts) == 1 else jnp.concatenate(parts, axis=1)


def _params(sem):
    return pltpu.CompilerParams(dimension_semantics=sem, vmem_limit_bytes=VMEM_LIMIT_BYTES)


def _ffn_kernel(x_ref, gain_ref, wgu_ref, wd_ref, o_ref, *, n_chunks, chunk):
    xf = x_ref[...]
    xn = _rms_norm_rows(xf, gain_ref[...]).astype(BF16)
    acc = xf
    for c in range(n_chunks):
        gu = _dot(xn, wgu_ref[c])
        act = (_silu(gu[:, :chunk]) * gu[:, chunk:]).astype(BF16)
        acc = acc + _dot(act, wd_ref[c])
    o_ref[...] = acc


def _ffn(x2d, gain, w_gate_up, w_down):
    n_rows = x2d.shape[0]
    n_chunks = D_FF // FFN_CHUNK
    wg = w_gate_up[:, :D_FF].reshape(D_MODEL, n_chunks, FFN_CHUNK)
    wu = w_gate_up[:, D_FF:].reshape(D_MODEL, n_chunks, FFN_CHUNK)
    wgu = jnp.concatenate([wg, wu], axis=-1).transpose(1, 0, 2).astype(BF16)
    wd = w_down.reshape(n_chunks, FFN_CHUNK, D_MODEL).astype(BF16)
    return pl.pallas_call(
        functools.partial(_ffn_kernel, n_chunks=n_chunks, chunk=FFN_CHUNK),
        grid=(n_rows // ROW_TILE,),
        in_specs=[
            pl.BlockSpec((ROW_TILE, D_MODEL), lambda i: (i, 0)),
            _const_spec((1, D_MODEL)),
            _const_spec((n_chunks, D_MODEL, 2 * FFN_CHUNK)),
            _const_spec((n_chunks, FFN_CHUNK, D_MODEL)),
        ],
        out_specs=pl.BlockSpec((ROW_TILE, D_MODEL), lambda i: (i, 0)),
        out_shape=jax.ShapeDtypeStruct((n_rows, D_MODEL), F32),
        compiler_params=_params(("parallel",)),
        name="ffn",
    )(x2d, gain.reshape(1, D_MODEL), wgu, wd)


QK_COLS = Q_W + 2 * KV_W


def _inproj0_kernel(x_ref, gain_ref, w_ref, qkgain_ref, cos_ref, sin_ref,
                    q_ref, k_ref, v_ref, gb_ref, cu_ref):
    xn = _rms_norm_rows(x_ref[...], gain_ref[...]).astype(BF16)
    qk = _dot(xn, w_ref[:, :QK_COLS])
    ss = _group_sumsq(qk, HEAD_DIM)
    qk = qk * lax.rsqrt(ss * (1.0 / HEAD_DIM) + EPS) * qkgain_ref[...]
    reps = QK_COLS // LANES
    cos = jnp.concatenate([cos_ref[...]] * reps, axis=1)
    sin = jnp.concatenate([sin_ref[...]] * reps, axis=1)
    half = HEAD_DIM // 2
    lane = lax.broadcasted_iota(jnp.int32, qk.shape, 1)
    partner = jnp.where((lane % HEAD_DIM) < half,
                        pltpu.roll(qk, QK_COLS - half, axis=1),
                        pltpu.roll(qk, half, axis=1))
    qk = qk * cos + partner * sin
    q_ref[...] = (qk[:, :Q_W] * (HEAD_DIM ** -0.5)).astype(BF16)
    k_ref[...] = qk[:, Q_W:].astype(BF16)
    c0 = QK_COLS
    v_ref[...] = _dot(xn, w_ref[:, c0:c0 + 2 * KV_W]).astype(BF16)
    c0 += 2 * KV_W
    gb_ref[...] = _dot(xn, w_ref[:, c0:c0 + CONV_CH]).astype(BF16)
    c0 += CONV_CH
    gc = _dot(xn, w_ref[:, c0:c0 + CONV_CH])
    xin = _dot(xn, w_ref[:, c0 + CONV_CH:c0 + 2 * CONV_CH])
    cu_ref[...] = (gc * xin).astype(BF16)


def _dup_heads(w, n_heads, dim):
    w3 = w.reshape(w.shape[0], n_heads, 1, dim)
    return jnp.concatenate([w3, w3], axis=2).reshape(w.shape[0], 2 * n_heads * dim)


def _inproj0(x2d, seq, gain, w_in, q_gain, k_gain):
    n_rows = x2d.shape[0]
    wq = w_in[:, :Q_W]
    wk = _dup_heads(w_in[:, Q_W:Q_W + KV_W], ATTN_KV_HEADS, HEAD_DIM)
    wv = _dup_heads(w_in[:, Q_W + KV_W:Q_W + 2 * KV_W], ATTN_KV_HEADS, HEAD_DIM)
    w = jnp.concatenate([wq, wk, wv, w_in[:, Q_W + 2 * KV_W:]], axis=1).astype(BF16)
    n_cols = w.shape[1]
    qkgain = jnp.concatenate([jnp.tile(q_gain, ATTN_HEADS), jnp.tile(k_gain, 2 * ATTN_KV_HEADS)]
                             ).reshape(1, QK_COLS).astype(F32)
    inv_freq = ROPE_THETA ** (-jnp.arange(0, HEAD_DIM, 2, dtype=F32) / HEAD_DIM)
    ang = jnp.arange(seq, dtype=F32)[:, None] * inv_freq[None, :]
    cos, sin = jnp.cos(ang), jnp.sin(ang)
    cos_t = jnp.tile(jnp.concatenate([cos, cos], axis=1), (1, LANES // HEAD_DIM))
    sin_t = jnp.tile(jnp.concatenate([-sin, sin], axis=1), (1, LANES // HEAD_DIM))
    tiles_per_seq = seq // ROW_TILE
    row = lambda i: (i, 0)
    pos = lambda i: (i % tiles_per_seq, 0)
    return pl.pallas_call(
        _inproj0_kernel,
        grid=(n_rows // ROW_TILE,),
        in_specs=[
            pl.BlockSpec((ROW_TILE, D_MODEL), row),
            _const_spec((1, D_MODEL)),
            _const_spec((D_MODEL, n_cols)),
            _const_spec((1, QK_COLS)),
            pl.BlockSpec((ROW_TILE, LANES), pos),
            pl.BlockSpec((ROW_TILE, LANES), pos),
        ],
        out_specs=[
            pl.BlockSpec((ROW_TILE, Q_W), row),
            pl.BlockSpec((ROW_TILE, 2 * KV_W), row),
            pl.BlockSpec((ROW_TILE, 2 * KV_W), row),
            pl.BlockSpec((ROW_TILE, CONV_CH), row),
            pl.BlockSpec((ROW_TILE, CONV_CH), row),
        ],
        out_shape=[
            jax.ShapeDtypeStruct((n_rows, Q_W), BF16),
            jax.ShapeDtypeStruct((n_rows, 2 * KV_W), BF16),
            jax.ShapeDtypeStruct((n_rows, 2 * KV_W), BF16),
            jax.ShapeDtypeStruct((n_rows, CONV_CH), BF16),
            jax.ShapeDtypeStruct((n_rows, CONV_CH), BF16),
        ],
        compiler_params=_params(("parallel",)),
        name="inproj0",
    )(x2d, gain.reshape(1, D_MODEL), w, qkgain, cos_t, sin_t)


CONV0_WIDTH = 3


def _mixer0_kernel(sinks_ref, q_ref, k_ref, v_ref, kp_ref, vp_ref, gb_ref, cu_ref, cup_ref,
                   x_ref, convw_ref, wout_ref, o_ref, y_ref, ext_ref):
    first = pl.program_id(1) == 0
    n_blocks = MIX_ROWS // ATTN_BLOCK
    grp = ATTN_HEADS // ATTN_KV_HEADS
    r = lax.broadcasted_iota(jnp.int32, (ATTN_BLOCK, 2 * ATTN_BLOCK), 0)
    c = lax.broadcasted_iota(jnp.int32, (ATTN_BLOCK, 2 * ATTN_BLOCK), 1)
    band = (c - r >= 1) & (c - r <= ATTN_BLOCK)
    lane = lax.broadcasted_iota(jnp.int32, (ATTN_BLOCK, LANES), 1)
    low_half = lane < HEAD_DIM
    for j in range(n_blocks):
        rows = slice(j * ATTN_BLOCK, (j + 1) * ATTN_BLOCK)
        if j == 0:
            k_prev, v_prev = kp_ref[...], vp_ref[...]
            valid = band & ((c >= ATTN_BLOCK) | jnp.logical_not(first))
        else:
            prev = slice((j - 1) * ATTN_BLOCK, j * ATTN_BLOCK)
            k_prev, v_prev = k_ref[prev, :], v_ref[prev, :]
            valid = band
        k2 = jnp.concatenate([k_prev, k_ref[rows, :]], axis=0)
        v2 = jnp.concatenate([v_prev, v_ref[rows, :]], axis=0)
        for pair in range(ATTN_HEADS // 2):
            kv = (2 * pair) // grp
            kk = k2[:, kv * LANES:(kv + 1) * LANES]
            vv = v2[:, kv * LANES:(kv + 1) * LANES]
            q2 = q_ref[rows, pair * LANES:(pair + 1) * LANES]
            outs = []
            for sub in range(2):
                head = 2 * pair + sub
                keep = low_half if sub == 0 else jnp.logical_not(low_half)
                qh = jnp.where(keep, q2, jnp.zeros_like(q2))
                s = _dot_nt(qh, kk)
                s = jnp.where(valid, s, NEG_BIG)
                sink = sinks_ref[head]
                m = jnp.maximum(jnp.max(s, axis=-1, keepdims=True), sink)
                p = jnp.exp(s - m)
                denom = jnp.sum(p, axis=-1, keepdims=True) + jnp.exp(sink - m)
                o = _dot(p.astype(BF16), vv)
                outs.append(o * (1.0 / denom))
            y_ref[rows, pair * LANES:(pair + 1) * LANES] = jnp.where(
                low_half, outs[0], outs[1]).astype(BF16)
    halo = SUBLANES
    prev_rows = cup_ref[...].astype(F32)
    ext_ref[0:halo, :] = jnp.where(first, jnp.zeros_like(prev_rows), prev_rows)
    ext_ref[halo:, :] = cu_ref[...].astype(F32)
    conv = jnp.zeros((MIX_ROWS, CONV_CH), F32)
    for t in range(CONV0_WIDTH):
        off = halo - (CONV0_WIDTH - 1) + t
        conv = conv + convw_ref[t:t + 1, :] * ext_ref[off:off + MIX_ROWS, :]
    y_ref[:, Q_W:] = (gb_ref[...].astype(F32) * conv).astype(BF16)
    o_ref[...] = x_ref[...] + _dot(y_ref[...], wout_ref[...])


def _mixer0(x2d, bsz, seq, q, k, v, gb, cu, sinks, conv_w, w_out):
    n_rows = x2d.shape[0]
    tiles = seq // MIX_ROWS
    blk_per_tile = MIX_ROWS // ATTN_BLOCK
    sub_per_tile = MIX_ROWS // SUBLANES
    row = lambda b, i: (b * tiles + i, 0)
    prev_blk = lambda b, i: (jnp.maximum((b * tiles + i) * blk_per_tile - 1, 0), 0)
    prev_sub = lambda b, i: (jnp.maximum((b * tiles + i) * sub_per_tile - 1, 0), 0)
    return pl.pallas_call(
        _mixer0_kernel,
        grid=(bsz, tiles),
        in_specs=[
            pl.BlockSpec(memory_space=pltpu.SMEM),
            pl.BlockSpec((MIX_ROWS, Q_W), row),
            pl.BlockSpec((MIX_ROWS, 2 * KV_W), row),
            pl.BlockSpec((MIX_ROWS, 2 * KV_W), row),
            pl.BlockSpec((ATTN_BLOCK, 2 * KV_W), prev_blk),
            pl.BlockSpec((ATTN_BLOCK, 2 * KV_W), prev_blk),
            pl.BlockSpec((MIX_ROWS, CONV_CH), row),
            pl.BlockSpec((MIX_ROWS, CONV_CH), row),
            pl.BlockSpec((SUBLANES, CONV_CH), prev_sub),
            pl.BlockSpec((MIX_ROWS, D_MODEL), row),
            _const_spec((CONV0_WIDTH, CONV_CH)),
            _const_spec((D_MODEL, D_MODEL)),
        ],
        out_specs=pl.BlockSpec((MIX_ROWS, D_MODEL), row),
        out_shape=jax.ShapeDtypeStruct((n_rows, D_MODEL), F32),
        scratch_shapes=[
            pltpu.VMEM((MIX_ROWS, D_MODEL), BF16),
            pltpu.VMEM((MIX_ROWS + SUBLANES, CONV_CH), F32),
        ],
        compiler_params=_params(("parallel", "arbitrary")),
        name="mixer0",
    )(sinks.astype(F32), q, k, v, k, v, gb, cu, cu, x2d, conv_w.astype(F32), w_out.astype(BF16))


GATE_LANES = LANES


def _inproj1_kernel(x_ref, gain_ref, w_ref, wab_ref, qkv_ref, z_ref, ab_ref):
    xn = _rms_norm_rows(x_ref[...], gain_ref[...]).astype(BF16)
    for c in range(3):
        cols = slice(c * DN_W, (c + 1) * DN_W)
        qkv_ref[:, cols] = _dot(xn, w_ref[:, cols]).astype(BF16)
    z_ref[...] = _dot(xn, w_ref[:, 3 * DN_W:]).astype(BF16)
    ab_ref[...] = _dot(xn, wab_ref[...])


def _inproj1(x2d, gain, w_in):
    n_rows = x2d.shape[0]
    w = w_in[:, :4 * DN_W].astype(BF16)
    wab = jnp.pad(w_in[:, 4 * DN_W:], ((0, 0), (0, GATE_LANES - 2 * DN_HEADS))).astype(BF16)
    row = lambda i: (i, 0)
    return pl.pallas_call(
        _inproj1_kernel,
        grid=(n_rows // ROW_TILE,),
        in_specs=[
            pl.BlockSpec((ROW_TILE, D_MODEL), row),
            _const_spec((1, D_MODEL)),
            _const_spec((D_MODEL, 4 * DN_W)),
            _const_spec((D_MODEL, GATE_LANES)),
        ],
        out_specs=[
            pl.BlockSpec((ROW_TILE, 3 * DN_W), row),
            pl.BlockSpec((ROW_TILE, DN_W), row),
            pl.BlockSpec((ROW_TILE, GATE_LANES), row),
        ],
        out_shape=[
            jax.ShapeDtypeStruct((n_rows, 3 * DN_W), BF16),
            jax.ShapeDtypeStruct((n_rows, DN_W), BF16),
            jax.ShapeDtypeStruct((n_rows, GATE_LANES), F32),
        ],
        compiler_params=_params(("parallel",)),
        name="inproj1",
    )(x2d, gain.reshape(1, D_MODEL), w, wab)


GDN_ROWS = MIX_ROWS
GDN_CHUNKS = GDN_ROWS // DN_CHUNK
SPLIT_TERMS = 3


def _split_bf16(v, terms):
    parts, rem = [], v
    for _ in range(terms):
        p = rem.astype(BF16)
        parts.append(p)
        rem = rem - p.astype(F32)
    return parts


def _softplus(v):
    return jnp.maximum(v, 0.0) + jnp.log(1.0 + jnp.exp(-jnp.abs(v)))


def _unit_lower_inverse(a_strict):
    n = a_strict.shape[0]
    r = lax.broadcasted_iota(jnp.int32, (n, n), 0)
    c = lax.broadcasted_iota(jnp.int32, (n, n), 1)
    b = -a_strict
    p = jnp.where(r == c, 1.0, 0.0) + b
    power = 2
    while power < n:
        b16 = b.astype(BF16)
        b = _dot(b16, b16)
        p = p + _dot(p.astype(BF16), b.astype(BF16))
        power *= 2
    return p


def _gdn_kernel(qkv_ref, z_ref, ab_ref, abt_ref, x_ref, convw_ref, gcol_ref, grow_ref, ogain_ref,
                wout_ref, o_ref, state_ref, halo_ref, ext_ref, act_ref, gate_ref, o_acc_ref):
    first = pl.program_id(1) == 0

    @pl.when(first)
    def _():
        state_ref[...] = jnp.zeros_like(state_ref)
        halo_ref[...] = jnp.zeros_like(halo_ref)

    halo = SUBLANES
    ext_ref[0:halo, :] = halo_ref[...]
    ext_ref[halo:, :] = qkv_ref[...].astype(F32)
    halo_ref[...] = ext_ref[GDN_ROWS:GDN_ROWS + halo, :]
    for col in range(3 * DN_HEADS):
        cols = slice(col * LANES, (col + 1) * LANES)
        conv = jnp.zeros((GDN_ROWS, LANES), F32)
        for t in range(DN_CONV_WIDTH):
            off = halo - (DN_CONV_WIDTH - 1) + t
            conv = conv + convw_ref[t:t + 1, cols] * ext_ref[off:off + GDN_ROWS, cols]
        act = _silu(conv)
        if col < 2 * DN_HEADS:
            ss = _dot((act * act).astype(BF16), jnp.ones((LANES, LANES), BF16))
            act = act * lax.rsqrt(ss + EPS)
            if col < DN_HEADS:
                act = act * (DN_HEAD_DIM ** -0.5)
        act_ref[:, cols] = act

    gcoef = gcol_ref[...]
    ab = ab_ref[...]
    gate_ref[:, 0:LANES] = gcoef[0:1, :] * _softplus(ab + gcoef[1:2, :])
    gate_ref[:, LANES:2 * LANES] = _sigmoid(ab)
    rcoef = grow_ref[...]

    ci = lax.broadcasted_iota(jnp.int32, (DN_CHUNK, DN_CHUNK), 0)
    cj = lax.broadcasted_iota(jnp.int32, (DN_CHUNK, DN_CHUNK), 1)
    lower = ci >= cj
    strict = ci > cj
    tril = lower.astype(BF16)
    triu = (ci <= cj).astype(BF16)

    def chunk_body(c, carry):
        rows = pl.ds(pl.multiple_of(c * DN_CHUNK, DN_CHUNK), DN_CHUNK)
        g_c = gate_ref[rows, 0:LANES]
        gc_col = sum(_dot(tril, p) for p in _split_bf16(g_c, SPLIT_TERMS))
        abt = abt_ref[c]
        g_r = rcoef[:, 0:1] * _softplus(abt + rcoef[:, 1:2])
        gc_row = sum(_dot(p, triu) for p in _split_bf16(g_r, SPLIT_TERMS))
        beta_c = gate_ref[rows, LANES:2 * LANES]
        for h in range(DN_HEADS):
            hc = slice(h * LANES, (h + 1) * LANES)
            q = act_ref[rows, hc]
            k = act_ref[rows, DN_W + h * LANES:DN_W + (h + 1) * LANES]
            v = act_ref[rows, 2 * DN_W + h * LANES:2 * DN_W + (h + 1) * LANES]
            beta = beta_c[:, h:h + 1]
            gcc = gc_col[:, DN_HEADS + h:DN_HEADS + h + 1]
            gcr = gc_row[DN_HEADS + h:DN_HEADS + h + 1, :]
            gc_last = gcc[DN_CHUNK - 1:DN_CHUNK, :]
            eg = jnp.exp(gcc)
            gamma = jnp.exp(jnp.where(lower, gcc - gcr, -jnp.inf))
            kb = k * beta
            k16 = k.astype(BF16)
            s1 = _dot_nt(jnp.concatenate([kb, q], axis=0).astype(BF16), k16)
            a_mat = jnp.where(strict, s1[:DN_CHUNK] * gamma, 0.0)
            attn = (s1[DN_CHUNK:] * gamma).astype(BF16)
            t_inv = _unit_lower_inverse(a_mat).astype(BF16)
            rhs = jnp.concatenate([v * beta, kb * eg], axis=1).astype(BF16)
            uw = _dot(t_inv, rhs).astype(BF16)
            kd = (k * jnp.exp(gc_last - gcc)).astype(BF16)
            nm = _dot_tn(kd, uw)
            oa = _dot(attn, uw)
            q_eff = q * eg - oa[:, LANES:]
            s_old = state_ref[h]
            res = _dot(jnp.concatenate([nm[:, LANES:], q_eff], axis=0).astype(BF16),
                       s_old.astype(BF16))
            state_ref[h] = jnp.exp(gc_last) * s_old - res[:DN_HEAD_DIM] + nm[:, :LANES]
            o_acc_ref[rows, hc] = res[DN_HEAD_DIM:] + oa[:, :LANES]
        return carry

    lax.fori_loop(0, GDN_CHUNKS, chunk_body, 0)

    o = o_acc_ref[...]
    ones = _block_ones(MXU_DIM, DN_HEAD_DIM)
    sq = (o * o).astype(BF16)
    ss = jnp.concatenate([_dot(sq[:, c0:c0 + MXU_DIM], ones) for c0 in range(0, DN_W, MXU_DIM)], axis=1)
    y = o * lax.rsqrt(ss * (1.0 / DN_HEAD_DIM) + EPS) * ogain_ref[...] * _silu(z_ref[...].astype(F32))
    o_ref[...] = x_ref[...] + _dot(y.astype(BF16), wout_ref[...])


def _layer1_mixer(x2d, bsz, seq, norm_gain, w_in, conv_w, a_log, dt_bias, o_gain, w_out):
    n_rows = x2d.shape[0]
    qkv, z, ab = _inproj1(x2d, norm_gain, w_in)
    n_chunks = n_rows // DN_CHUNK
    abt = ab[:, :2 * DN_HEADS].reshape(n_chunks, DN_CHUNK, 2 * DN_HEADS).transpose(0, 2, 1)
    neg_a = -jnp.exp(a_log.astype(F32))
    pad_lo = jnp.zeros((DN_HEADS,), F32)
    gcol = jnp.zeros((2, LANES), F32)
    gcol = gcol.at[0, DN_HEADS:2 * DN_HEADS].set(neg_a).at[1, DN_HEADS:2 * DN_HEADS].set(dt_bias.astype(F32))
    grow = jnp.stack([jnp.concatenate([pad_lo, neg_a]), jnp.concatenate([pad_lo, dt_bias.astype(F32)])], axis=1)
    ogain = jnp.tile(o_gain.astype(F32), DN_HEADS).reshape(1, DN_W)
    tiles = seq // GDN_ROWS
    row = lambda b, i: (b * tiles + i, 0)
    return pl.pallas_call(
        _gdn_kernel,
        grid=(bsz, tiles),
        in_specs=[
            pl.BlockSpec((GDN_ROWS, 3 * DN_W), row),
            pl.BlockSpec((GDN_ROWS, DN_W), row),
            pl.BlockSpec((GDN_ROWS, GATE_LANES), row),
            pl.BlockSpec((GDN_CHUNKS, 2 * DN_HEADS, DN_CHUNK), lambda b, i: (b * tiles + i, 0, 0)),
            pl.BlockSpec((GDN_ROWS, D_MODEL), row),
            _const_spec((DN_CONV_WIDTH, 3 * DN_W)),
            _const_spec((2, LANES)),
            _const_spec((2 * DN_HEADS, 2)),
            _const_spec((1, DN_W)),
            _const_spec((DN_W, D_MODEL)),
        ],
        out_specs=pl.BlockSpec((GDN_ROWS, D_MODEL), row),
        out_shape=jax.ShapeDtypeStruct((n_rows, D_MODEL), F32),
        scratch_shapes=[
            pltpu.VMEM((DN_HEADS, DN_HEAD_DIM, DN_HEAD_DIM), F32),
            pltpu.VMEM((SUBLANES, 3 * DN_W), F32),
            pltpu.VMEM((GDN_ROWS + SUBLANES, 3 * DN_W), F32),
            pltpu.VMEM((GDN_ROWS, 3 * DN_W), F32),
            pltpu.VMEM((GDN_ROWS, 2 * LANES), F32),
            pltpu.VMEM((GDN_ROWS, DN_W), F32),
        ],
        compiler_params=_params(("parallel", "arbitrary")),
        name="gdn",
    )(qkv, z, ab, abt, x2d, conv_w.astype(F32), gcol, grow, ogain, w_out.astype(BF16))


def kernel(x, even_norm, even_w_in, even_q_gain, even_k_gain, even_sinks, even_conv_w, even_w_out, odd_norm, odd_w_in, odd_conv_w, odd_a_log, odd_dt_bias, odd_o_gain, odd_w_out, ffn_norm, ffn_w_gate_up, ffn_w_down):
    bsz, seq, _ = x.shape
    x2d = x.reshape(bsz * seq, D_MODEL)
    q, k, v, gb, cu = _inproj0(x2d, seq, even_norm[0], even_w_in[0], even_q_gain[0], even_k_gain[0])
    x2d = _mixer0(x2d, bsz, seq, q, k, v, gb, cu, even_sinks[0], even_conv_w[0], even_w_out[0])
    x2d = _ffn(x2d, ffn_norm[0], ffn_w_gate_up[0], ffn_w_down[0])
    x2d = _layer1_mixer(x2d, bsz, seq, odd_norm[0], odd_w_in[0], odd_conv_w[0], odd_a_log[0],
                        odd_dt_bias[0], odd_o_gain[0], odd_w_out[0])
    x2d = _ffn(x2d, ffn_norm[1], ffn_w_gate_up[1], ffn_w_down[1])
    return x2d.reshape(bsz, seq, D_MODEL)
```

```python
import functools

import jax
import jax.numpy as jnp
from jax import lax
from jax.experimental import pallas as pl
from jax.experimental.pallas import tpu as pltpu

D_MODEL = 1024
D_FF = 2816
EPS = 1e-6

HEAD_DIM = 64
ATTN_HEADS = 8
ATTN_KV_HEADS = 2
ATTN_BLOCK = 128
ROPE_THETA = 10000.0
CONV_CH = 512
Q_W = ATTN_HEADS * HEAD_DIM
KV_W = ATTN_KV_HEADS * HEAD_DIM
DN_HEAD_DIM = 128
DN_HEADS = 8
DN_W = DN_HEADS * DN_HEAD_DIM
DN_CONV_WIDTH = 4
DN_CHUNK = 64

LANES = 128
SUBLANES = 8
MXU_DIM = 256
VMEM_LIMIT_BYTES = 56 * 1024 * 1024

ROW_TILE = 512
FFN_CHUNK = 1408
MIX_ROWS = 512
NEG_BIG = -1e30

BF16 = jnp.bfloat16
F32 = jnp.float32


def _rms_norm_rows(xf, gain):
    ms = jnp.mean(xf * xf, axis=-1, keepdims=True)
    return xf * lax.rsqrt(ms + EPS) * gain


def _sigmoid(v):
    return 1.0 / (1.0 + jnp.exp(-v))


def _silu(v):
    return v * _sigmoid(v)


def _dot(a, b):
    return jnp.dot(a, b, preferred_element_type=F32)


def _dot_nt(a, b):
    return lax.dot_general(a, b, (((1,), (1,)), ((), ())), preferred_element_type=F32)


def _dot_tn(a, b):
    return lax.dot_general(a, b, (((0,), (0,)), ((), ())), preferred_element_type=F32)


def _const_spec(shape):
    nd = len(shape)
    return pl.BlockSpec(shape, lambda *_: (0,) * nd, pipeline_mode=pl.Buffered(1))


def _block_ones(n, block):
    r = lax.broadcasted_iota(jnp.int32, (n, n), 0) // block
    c = lax.broadcasted_iota(jnp.int32, (n, n), 1) // block
    return (r == c).astype(BF16)


def _group_sumsq(v, block):
    ones = _block_ones(MXU_DIM, block)
    sq = (v * v).astype(BF16)
    parts = [_dot(sq[:, c:c + MXU_DIM], ones) for c in range(0, v.shape[1], MXU_DIM)]
    return parts[0] if len(parts) == 1 else jnp.concatenate(parts, axis=1)


def _params(sem):
    return pltpu.CompilerParams(dimension_semantics=sem, vmem_limit_bytes=VMEM_LIMIT_BYTES)


def _ffn_kernel(x_ref, gain_ref, wgu_ref, wd_ref, o_ref, *, n_chunks, chunk):
    xf = x_ref[...]
    xn = _rms_norm_rows(xf, gain_ref[...]).astype(BF16)
    acc = xf
    for c in range(n_chunks):
        gu = _dot(xn, wgu_ref[c])
        act = (_silu(gu[:, :chunk]) * gu[:, chunk:]).astype(BF16)
        acc = acc + _dot(act, wd_ref[c])
    o_ref[...] = acc


def _ffn(x2d, gain, w_gate_up, w_down):
    n_rows = x2d.shape[0]
    n_chunks = D_FF // FFN_CHUNK
    wg = w_gate_up[:, :D_FF].reshape(D_MODEL, n_chunks, FFN_CHUNK)
    wu = w_gate_up[:, D_FF:].reshape(D_MODEL, n_chunks, FFN_CHUNK)
    wgu = jnp.concatenate([wg, wu], axis=-1).transpose(1, 0, 2).astype(BF16)
    wd = w_down.reshape(n_chunks, FFN_CHUNK, D_MODEL).astype(BF16)
    return pl.pallas_call(
        functools.partial(_ffn_kernel, n_chunks=n_chunks, chunk=FFN_CHUNK),
        grid=(n_rows // ROW_TILE,),
        in_specs=[
            pl.BlockSpec((ROW_TILE, D_MODEL), lambda i: (i, 0)),
            _const_spec((1, D_MODEL)),
            _const_spec((n_chunks, D_MODEL, 2 * FFN_CHUNK)),
            _const_spec((n_chunks, FFN_CHUNK, D_MODEL)),
        ],
        out_specs=pl.BlockSpec((ROW_TILE, D_MODEL), lambda i: (i, 0)),
        out_shape=jax.ShapeDtypeStruct((n_rows, D_MODEL), F32),
        compiler_params=_params(("parallel",)),
        name="ffn",
    )(x2d, gain.reshape(1, D_MODEL), wgu, wd)


QK_COLS = Q_W + 2 * KV_W


def _inproj0_kernel(x_ref, gain_ref, w_ref, qkgain_ref, cos_ref, sin_ref,
                    q_ref, k_ref, v_ref, gb_ref, cu_ref):
    xn = _rms_norm_rows(x_ref[...], gain_ref[...]).astype(BF16)
    qk = _dot(xn, w_ref[:, :QK_COLS])
    ss = _group_sumsq(qk, HEAD_DIM)
    qk = qk * lax.rsqrt(ss * (1.0 / HEAD_DIM) + EPS) * qkgain_ref[...]
    reps = QK_COLS // LANES
    cos = jnp.concatenate([cos_ref[...]] * reps, axis=1)
    sin = jnp.concatenate([sin_ref[...]] * reps, axis=1)
    half = HEAD_DIM // 2
    lane = lax.broadcasted_iota(jnp.int32, qk.shape, 1)
    partner = jnp.where((lane % HEAD_DIM) < half,
                        pltpu.roll(qk, QK_COLS - half, axis=1),
                        pltpu.roll(qk, half, axis=1))
    qk = qk * cos + partner * sin
    q_ref[...] = (qk[:, :Q_W] * (HEAD_DIM ** -0.5)).astype(BF16)
    k_ref[...] = qk[:, Q_W:].astype(BF16)
    c0 = QK_COLS
    v_ref[...] = _dot(xn, w_ref[:, c0:c0 + 2 * KV_W]).astype(BF16)
    c0 += 2 * KV_W
    gb_ref[...] = _dot(xn, w_ref[:, c0:c0 + CONV_CH]).astype(BF16)
    c0 += CONV_CH
    gc = _dot(xn, w_ref[:, c0:c0 + CONV_CH])
    xin = _dot(xn, w_ref[:, c0 + CONV_CH:c0 + 2 * CONV_CH])
    cu_ref[...] = (gc * xin).astype(BF16)


def _dup_heads(w, n_heads, dim):
    w3 = w.reshape(w.shape[0], n_heads, 1, dim)
    return jnp.concatenate([w3, w3], axis=2).reshape(w.shape[0], 2 * n_heads * dim)


def _inproj0(x2d, seq, gain, w_in, q_gain, k_gain):
    n_rows = x2d.shape[0]
    wq = w_in[:, :Q_W]
    wk = _dup_heads(w_in[:, Q_W:Q_W + KV_W], ATTN_KV_HEADS, HEAD_DIM)
    wv = _dup_heads(w_in[:, Q_W + KV_W:Q_W + 2 * KV_W], ATTN_KV_HEADS, HEAD_DIM)
    w = jnp.concatenate([wq, wk, wv, w_in[:, Q_W + 2 * KV_W:]], axis=1).astype(BF16)
    n_cols = w.shape[1]
    qkgain = jnp.concatenate([jnp.tile(q_gain, ATTN_HEADS), jnp.tile(k_gain, 2 * ATTN_KV_HEADS)]
                             ).reshape(1, QK_COLS).astype(F32)
    inv_freq = ROPE_THETA ** (-jnp.arange(0, HEAD_DIM, 2, dtype=F32) / HEAD_DIM)
    ang = jnp.arange(seq, dtype=F32)[:, None] * inv_freq[None, :]
    cos, sin = jnp.cos(ang), jnp.sin(ang)
    cos_t = jnp.tile(jnp.concatenate([cos, cos], axis=1), (1, LANES // HEAD_DIM))
    sin_t = jnp.tile(jnp.concatenate([-sin, sin], axis=1), (1, LANES // HEAD_DIM))
    tiles_per_seq = seq // ROW_TILE
    row = lambda i: (i, 0)
    pos = lambda i: (i % tiles_per_seq, 0)
    return pl.pallas_call(
        _inproj0_kernel,
        grid=(n_rows // ROW_TILE,),
        in_specs=[
            pl.BlockSpec((ROW_TILE, D_MODEL), row),
            _const_spec((1, D_MODEL)),
            _const_spec((D_MODEL, n_cols)),
            _const_spec((1, QK_COLS)),
            pl.BlockSpec((ROW_TILE, LANES), pos),
            pl.BlockSpec((ROW_TILE, LANES), pos),
        ],
        out_specs=[
            pl.BlockSpec((ROW_TILE, Q_W), row),
            pl.BlockSpec((ROW_TILE, 2 * KV_W), row),
            pl.BlockSpec((ROW_TILE, 2 * KV_W), row),
            pl.BlockSpec((ROW_TILE, CONV_CH), row),
            pl.BlockSpec((ROW_TILE, CONV_CH), row),
        ],
        out_shape=[
            jax.ShapeDtypeStruct((n_rows, Q_W), BF16),
            jax.ShapeDtypeStruct((n_rows, 2 * KV_W), BF16),
            jax.ShapeDtypeStruct((n_rows, 2 * KV_W), BF16),
            jax.ShapeDtypeStruct((n_rows, CONV_CH), BF16),
            jax.ShapeDtypeStruct((n_rows, CONV_CH), BF16),
        ],
        compiler_params=_params(("parallel",)),
        name="inproj0",
    )(x2d, gain.reshape(1, D_MODEL), w, qkgain, cos_t, sin_t)


CONV0_WIDTH = 3


def _mixer0_kernel(sinks_ref, q_ref, k_ref, v_ref, kp_ref, vp_ref, gb_ref, cu_ref, cup_ref,
                   x_ref, convw_ref, wout_ref, o_ref, y_ref, ext_ref):
    first = pl.program_id(1) == 0
    n_blocks = MIX_ROWS // ATTN_BLOCK
    grp = ATTN_HEADS // ATTN_KV_HEADS
    r = lax.broadcasted_iota(jnp.int32, (ATTN_BLOCK, 2 * ATTN_BLOCK), 0)
    c = lax.broadcasted_iota(jnp.int32, (ATTN_BLOCK, 2 * ATTN_BLOCK), 1)
    band = (c - r >= 1) & (c - r <= ATTN_BLOCK)
    lane = lax.broadcasted_iota(jnp.int32, (ATTN_BLOCK, LANES), 1)
    low_half = lane < HEAD_DIM
    for j in range(n_blocks):
        rows = slice(j * ATTN_BLOCK, (j + 1) * ATTN_BLOCK)
        if j == 0:
            k_prev, v_prev = kp_ref[...], vp_ref[...]
            valid = band & ((c >= ATTN_BLOCK) | jnp.logical_not(first))
        else:
            prev = slice((j - 1) * ATTN_BLOCK, j * ATTN_BLOCK)
            k_prev, v_prev = k_ref[prev, :], v_ref[prev, :]
            valid = band
        k2 = jnp.concatenate([k_prev, k_ref[rows, :]], axis=0)
        v2 = jnp.concatenate([v_prev, v_ref[rows, :]], axis=0)
        for pair in range(ATTN_HEADS // 2):
            kv = (2 * pair) // grp
            kk = k2[:, kv * LANES:(kv + 1) * LANES]
            vv = v2[:, kv * LANES:(kv + 1) * LANES]
            q2 = q_ref[rows, pair * LANES:(pair + 1) * LANES]
            outs = []
            for sub in range(2):
                head = 2 * pair + sub
                keep = low_half if sub == 0 else jnp.logical_not(low_half)
                qh = jnp.where(keep, q2, jnp.zeros_like(q2))
                s = _dot_nt(qh, kk)
                s = jnp.where(valid, s, NEG_BIG)
                sink = sinks_ref[head]
                m = jnp.maximum(jnp.max(s, axis=-1, keepdims=True), sink)
                p = jnp.exp(s - m)
                denom = jnp.sum(p, axis=-1, keepdims=True) + jnp.exp(sink - m)
                o = _dot(p.astype(BF16), vv)
                outs.append(o * (1.0 / denom))
            y_ref[rows, pair * LANES:(pair + 1) * LANES] = jnp.where(
                low_half, outs[0], outs[1]).astype(BF16)
    halo = SUBLANES
    prev_rows = cup_ref[...].astype(F32)
    ext_ref[0:halo, :] = jnp.where(first, jnp.zeros_like(prev_rows), prev_rows)
    ext_ref[halo:, :] = cu_ref[...].astype(F32)
    conv = jnp.zeros((MIX_ROWS, CONV_CH), F32)
    for t in range(CONV0_WIDTH):
        off = halo - (CONV0_WIDTH - 1) + t
        conv = conv + convw_ref[t:t + 1, :] * ext_ref[off:off + MIX_ROWS, :]
    y_ref[:, Q_W:] = (gb_ref[...].astype(F32) * conv).astype(BF16)
    o_ref[...] = x_ref[...] + _dot(y_ref[...], wout_ref[...])


def _mixer0(x2d, bsz, seq, q, k, v, gb, cu, sinks, conv_w, w_out):
    n_rows = x2d.shape[0]
    tiles = seq // MIX_ROWS
    blk_per_tile = MIX_ROWS // ATTN_BLOCK
    sub_per_tile = MIX_ROWS // SUBLANES
    row = lambda b, i: (b * tiles + i, 0)
    prev_blk = lambda b, i: (jnp.maximum((b * tiles + i) * blk_per_tile - 1, 0), 0)
    prev_sub = lambda b, i: (jnp.maximum((b * tiles + i) * sub_per_tile - 1, 0), 0)
    return pl.pallas_call(
        _mixer0_kernel,
        grid=(bsz, tiles),
        in_specs=[
            pl.BlockSpec(memory_space=pltpu.SMEM),
            pl.BlockSpec((MIX_ROWS, Q_W), row),
            pl.BlockSpec((MIX_ROWS, 2 * KV_W), row),
            pl.BlockSpec((MIX_ROWS, 2 * KV_W), row),
            pl.BlockSpec((ATTN_BLOCK, 2 * KV_W), prev_blk),
            pl.BlockSpec((ATTN_BLOCK, 2 * KV_W), prev_blk),
            pl.BlockSpec((MIX_ROWS, CONV_CH), row),
            pl.BlockSpec((MIX_ROWS, CONV_CH), row),
            pl.BlockSpec((SUBLANES, CONV_CH), prev_sub),
            pl.BlockSpec((MIX_ROWS, D_MODEL), row),
            _const_spec((CONV0_WIDTH, CONV_CH)),
            _const_spec((D_MODEL, D_MODEL)),
        ],
        out_specs=pl.BlockSpec((MIX_ROWS, D_MODEL), row),
        out_shape=jax.ShapeDtypeStruct((n_rows, D_MODEL), F32),
        scratch_shapes=[
            pltpu.VMEM((MIX_ROWS, D_MODEL), BF16),
            pltpu.VMEM((MIX_ROWS + SUBLANES, CONV_CH), F32),
        ],
        compiler_params=_params(("parallel", "arbitrary")),
        name="mixer0",
    )(sinks.astype(F32), q, k, v, k, v, gb, cu, cu, x2d, conv_w.astype(F32), w_out.astype(BF16))


GATE_LANES = LANES


def _inproj1_kernel(x_ref, gain_ref, w_ref, wab_ref, qkv_ref, z_ref, ab_ref):
    xn = _rms_norm_rows(x_ref[...], gain_ref[...]).astype(BF16)
    for c in range(3):
        cols = slice(c * DN_W, (c + 1) * DN_W)
        qkv_ref[:, cols] = _dot(xn, w_ref[:, cols]).astype(BF16)
    z_ref[...] = _dot(xn, w_ref[:, 3 * DN_W:]).astype(BF16)
    ab_ref[...] = _dot(xn, wab_ref[...])


def _inproj1(x2d, gain, w_in):
    n_rows = x2d.shape[0]
    w = w_in[:, :4 * DN_W].astype(BF16)
    wab = jnp.pad(w_in[:, 4 * DN_W:], ((0, 0), (0, GATE_LANES - 2 * DN_HEADS))).astype(BF16)
    row = lambda i: (i, 0)
    return pl.pallas_call(
        _inproj1_kernel,
        grid=(n_rows // ROW_TILE,),
        in_specs=[
            pl.BlockSpec((ROW_TILE, D_MODEL), row),
            _const_spec((1, D_MODEL)),
            _const_spec((D_MODEL, 4 * DN_W)),
            _const_spec((D_MODEL, GATE_LANES)),
        ],
        out_specs=[
            pl.BlockSpec((ROW_TILE, 3 * DN_W), row),
            pl.BlockSpec((ROW_TILE, DN_W), row),
            pl.BlockSpec((ROW_TILE, GATE_LANES), row),
        ],
        out_shape=[
            jax.ShapeDtypeStruct((n_rows, 3 * DN_W), BF16),
            jax.ShapeDtypeStruct((n_rows, DN_W), BF16),
            jax.ShapeDtypeStruct((n_rows, GATE_LANES), F32),
        ],
        compiler_params=_params(("parallel",)),
        name="inproj1",
    )(x2d, gain.reshape(1, D_MODEL), w, wab)


GDN_ROWS = MIX_ROWS
GDN_CHUNKS = GDN_ROWS // DN_CHUNK
SPLIT_TERMS = 3


def _split_bf16(v, terms):
    parts, rem = [], v
    for _ in range(terms):
        p = rem.astype(BF16)
        parts.append(p)
        rem = rem - p.astype(F32)
    return parts


def _softplus(v):
    return jnp.maximum(v, 0.0) + jnp.log(1.0 + jnp.exp(-jnp.abs(v)))


def _gdn_kernel(qkv_ref, z_ref, ab_ref, abt_ref, x_ref, convw_ref, gcol_ref, grow_ref, ogain_ref,
                wout_ref, o_ref, state_ref, halo_ref, ext_ref, act_ref, gate_ref, o_acc_ref):
    first = pl.program_id(1) == 0

    @pl.when(first)
    def _():
        state_ref[...] = jnp.zeros_like(state_ref)
        halo_ref[...] = jnp.zeros_like(halo_ref)

    halo = SUBLANES
    ext_ref[0:halo, :] = halo_ref[...]
    ext_ref[halo:, :] = qkv_ref[...].astype(F32)
    halo_ref[...] = ext_ref[GDN_ROWS:GDN_ROWS + halo, :]
    for col in range(3 * DN_HEADS):
        cols = slice(col * LANES, (col + 1) * LANES)
        conv = jnp.zeros((GDN_ROWS, LANES), F32)
        for t in range(DN_CONV_WIDTH):
            off = halo - (DN_CONV_WIDTH - 1) + t
            conv = conv + convw_ref[t:t + 1, cols] * ext_ref[off:off + GDN_ROWS, cols]
        act = _silu(conv)
        if col < 2 * DN_HEADS:
            ss = _dot((act * act).astype(BF16), jnp.ones((LANES, LANES), BF16))
            act = act * lax.rsqrt(ss + EPS)
            if col < DN_HEADS:
                act = act * (DN_HEAD_DIM ** -0.5)
        act_ref[:, cols] = act

    gcoef = gcol_ref[...]
    ab = ab_ref[...]
    gate_ref[:, 0:LANES] = gcoef[0:1, :] * _softplus(ab + gcoef[1:2, :])
    gate_ref[:, LANES:2 * LANES] = _sigmoid(ab)
    rcoef = grow_ref[...]

    ci = lax.broadcasted_iota(jnp.int32, (DN_CHUNK, DN_CHUNK), 0)
    cj = lax.broadcasted_iota(jnp.int32, (DN_CHUNK, DN_CHUNK), 1)
    lower = ci >= cj
    strict = ci > cj
    tril = lower.astype(BF16)
    triu = (ci <= cj).astype(BF16)

    def chunk_body(c, carry):
        rows = pl.ds(pl.multiple_of(c * DN_CHUNK, DN_CHUNK), DN_CHUNK)
        g_c = gate_ref[rows, 0:LANES]
        gc_col = sum(_dot(tril, p) for p in _split_bf16(g_c, SPLIT_TERMS))
        abt = abt_ref[c]
        g_r = rcoef[:, 0:1] * _softplus(abt + rcoef[:, 1:2])
        gc_row = sum(_dot(p, triu) for p in _split_bf16(g_r, SPLIT_TERMS))
        beta_c = gate_ref[rows, LANES:2 * LANES]
        heads = range(DN_HEADS)
        q = [act_ref[rows, h * LANES:(h + 1) * LANES] for h in heads]
        k = [act_ref[rows, DN_W + h * LANES:DN_W + (h + 1) * LANES] for h in heads]
        v = [act_ref[rows, 2 * DN_W + h * LANES:2 * DN_W + (h + 1) * LANES] for h in heads]
        beta = [beta_c[:, h:h + 1] for h in heads]
        gcc = [gc_col[:, DN_HEADS + h:DN_HEADS + h + 1] for h in heads]
        gcr = [gc_row[DN_HEADS + h:DN_HEADS + h + 1, :] for h in heads]
        gc_last = [g[DN_CHUNK - 1:DN_CHUNK, :] for g in gcc]
        eg = [jnp.exp(g) for g in gcc]
        gamma = [jnp.exp(jnp.where(lower, gcc[h] - gcr[h], -jnp.inf)) for h in heads]
        kb = [k[h] * beta[h] for h in heads]
        s1 = [_dot_nt(jnp.concatenate([kb[h], q[h]], axis=0).astype(BF16), k[h].astype(BF16))
              for h in heads]
        attn = [(s1[h][DN_CHUNK:] * gamma[h]).astype(BF16) for h in heads]
        b = [jnp.where(strict, -(s1[h][:DN_CHUNK] * gamma[h]), 0.0).astype(BF16) for h in heads]
        p = [jnp.where(ci == cj, 1.0, 0.0) + b[h].astype(F32) for h in heads]
        b = [_dot(b[h], b[h]).astype(BF16) for h in heads]
        power = 2
        while power < DN_CHUNK:
            last = 2 * power >= DN_CHUNK
            lhs = [p[h].astype(BF16) if last else jnp.concatenate([b[h], p[h].astype(BF16)], axis=0)
                   for h in heads]
            prod = [_dot(lhs[h], b[h]) for h in heads]
            if last:
                p = [p[h] + prod[h] for h in heads]
            else:
                b = [prod[h][:DN_CHUNK].astype(BF16) for h in heads]
                p = [p[h] + prod[h][DN_CHUNK:] for h in heads]
            power *= 2
        rhs = [jnp.concatenate([v[h] * beta[h], kb[h] * eg[h]], axis=1).astype(BF16) for h in heads]
        uw = [_dot(p[h].astype(BF16), rhs[h]).astype(BF16) for h in heads]
        kd = [(k[h] * jnp.exp(gc_last[h] - gcc[h])).astype(BF16) for h in heads]
        nm = [_dot_tn(kd[h], uw[h]) for h in heads]
        oa = [_dot(attn[h], uw[h]) for h in heads]
        s_old = [state_ref[h] for h in heads]
        lhs = [jnp.concatenate([nm[h][:, LANES:], q[h] * eg[h] - oa[h][:, LANES:]], axis=0).astype(BF16)
               for h in heads]
        res = [_dot(lhs[h], s_old[h].astype(BF16)) for h in heads]
        for h in heads:
            state_ref[h] = jnp.exp(gc_last[h]) * s_old[h] - res[h][:DN_HEAD_DIM] + nm[h][:, :LANES]
            o_acc_ref[rows, h * LANES:(h + 1) * LANES] = res[h][DN_HEAD_DIM:] + oa[h][:, :LANES]
        return carry

    lax.fori_loop(0, GDN_CHUNKS, chunk_body, 0)

    o = o_acc_ref[...]
    ones = _block_ones(MXU_DIM, DN_HEAD_DIM)
    sq = (o * o).astype(BF16)
    ss = jnp.concatenate([_dot(sq[:, c0:c0 + MXU_DIM], ones) for c0 in range(0, DN_W, MXU_DIM)], axis=1)
    y = o * lax.rsqrt(ss * (1.0 / DN_HEAD_DIM) + EPS) * ogain_ref[...] * _silu(z_ref[...].astype(F32))
    o_ref[...] = x_ref[...] + _dot(y.astype(BF16), wout_ref[...])


def _layer1_mixer(x2d, bsz, seq, norm_gain, w_in, conv_w, a_log, dt_bias, o_gain, w_out):
    n_rows = x2d.shape[0]
    qkv, z, ab = _inproj1(x2d, norm_gain, w_in)
    n_chunks = n_rows // DN_CHUNK
    abt = ab[:, :2 * DN_HEADS].reshape(n_chunks, DN_CHUNK, 2 * DN_HEADS).transpose(0, 2, 1)
    neg_a = -jnp.exp(a_log.astype(F32))
    pad_lo = jnp.zeros((DN_HEADS,), F32)
    gcol = jnp.zeros((2, LANES), F32)
    gcol = gcol.at[0, DN_HEADS:2 * DN_HEADS].set(neg_a).at[1, DN_HEADS:2 * DN_HEADS].set(dt_bias.astype(F32))
    grow = jnp.stack([jnp.concatenate([pad_lo, neg_a]), jnp.concatenate([pad_lo, dt_bias.astype(F32)])], axis=1)
    ogain = jnp.tile(o_gain.astype(F32), DN_HEADS).reshape(1, DN_W)
    tiles = seq // GDN_ROWS
    row = lambda b, i: (b * tiles + i, 0)
    return pl.pallas_call(
        _gdn_kernel,
        grid=(bsz, tiles),
        in_specs=[
            pl.BlockSpec((GDN_ROWS, 3 * DN_W), row),
            pl.BlockSpec((GDN_ROWS, DN_W), row),
            pl.BlockSpec((GDN_ROWS, GATE_LANES), row),
            pl.BlockSpec((GDN_CHUNKS, 2 * DN_HEADS, DN_CHUNK), lambda b, i: (b * tiles + i, 0, 0)),
            pl.BlockSpec((GDN_ROWS, D_MODEL), row),
            _const_spec((DN_CONV_WIDTH, 3 * DN_W)),
            _const_spec((2, LANES)),
            _const_spec((2 * DN_HEADS, 2)),
            _const_spec((1, DN_W)),
            _const_spec((DN_W, D_MODEL)),
        ],
        out_specs=pl.BlockSpec((GDN_ROWS, D_MODEL), row),
        out_shape=jax.ShapeDtypeStruct((n_rows, D_MODEL), F32),
        scratch_shapes=[
            pltpu.VMEM((DN_HEADS, DN_HEAD_DIM, DN_HEAD_DIM), F32),
            pltpu.VMEM((SUBLANES, 3 * DN_W), F32),
            pltpu.VMEM((GDN_ROWS + SUBLANES, 3 * DN_W), F32),
            pltpu.VMEM((GDN_ROWS, 3 * DN_W), F32),
            pltpu.VMEM((GDN_ROWS, 2 * LANES), F32),
            pltpu.VMEM((GDN_ROWS, DN_W), F32),
        ],
        compiler_params=_params(("parallel", "arbitrary")),
        name="gdn",
    )(qkv, z, ab, abt, x2d, conv_w.astype(F32), gcol, grow, ogain, w_out.astype(BF16))


def kernel(x, even_norm, even_w_in, even_q_gain, even_k_gain, even_sinks, even_conv_w, even_w_out, odd_norm, odd_w_in, odd_conv_w, odd_a_log, odd_dt_bias, odd_o_gain, odd_w_out, ffn_norm, ffn_w_gate_up, ffn_w_down):
    bsz, seq, _ = x.shape
    x2d = x.reshape(bsz * seq, D_MODEL)
    q, k, v, gb, cu = _inproj0(x2d, seq, even_norm[0], even_w_in[0], even_q_gain[0], even_k_gain[0])
    x2d = _mixer0(x2d, bsz, seq, q, k, v, gb, cu, even_sinks[0], even_conv_w[0], even_w_out[0])
    x2d = _ffn(x2d, ffn_norm[0], ffn_w_gate_up[0], ffn_w_down[0])
    x2d = _layer1_mixer(x2d, bsz, seq, odd_norm[0], odd_w_in[0], odd_conv_w[0], odd_a_log[0],
                        odd_dt_bias[0], odd_o_gain[0], odd_w_out[0])
    x2d = _ffn(x2d, ffn_norm[1], ffn_w_gate_up[1], ffn_w_down[1])
    return x2d.reshape(bsz, seq, D_MODEL)
```

```python
import functools

import jax
import jax.numpy as jnp
from jax import lax
from jax.experimental import pallas as pl
from jax.experimental.pallas import tpu as pltpu

D_MODEL = 1024
D_FF = 2816
EPS = 1e-6

HEAD_DIM = 64
ATTN_HEADS = 8
ATTN_KV_HEADS = 2
ATTN_BLOCK = 128
ROPE_THETA = 10000.0
CONV_CH = 512
Q_W = ATTN_HEADS * HEAD_DIM
KV_W = ATTN_KV_HEADS * HEAD_DIM
DN_HEAD_DIM = 128
DN_HEADS = 8
DN_W = DN_HEADS * DN_HEAD_DIM
DN_CONV_WIDTH = 4
DN_CHUNK = 64

LANES = 128
SUBLANES = 8
MXU_DIM = 256
VMEM_LIMIT_BYTES = 56 * 1024 * 1024

ROW_TILE = 512
FFN_SPLITS = (0, 1536, D_FF)
MIX_ROWS = 512
NEG_BIG = -1e30

BF16 = jnp.bfloat16
F32 = jnp.float32


def _rms_norm_rows(xf, gain):
    ms = jnp.mean(xf * xf, axis=-1, keepdims=True)
    return xf * lax.rsqrt(ms + EPS) * gain


def _sigmoid(v):
    return 1.0 / (1.0 + jnp.exp(-v))


def _silu(v):
    return v * _sigmoid(v)


def _dot(a, b):
    return jnp.dot(a, b, preferred_element_type=F32)


def _dot_nt(a, b):
    return lax.dot_general(a, b, (((1,), (1,)), ((), ())), preferred_element_type=F32)


def _dot_tn(a, b):
    return lax.dot_general(a, b, (((0,), (0,)), ((), ())), preferred_element_type=F32)


def _const_spec(shape):
    nd = len(shape)
    return pl.BlockSpec(shape, lambda *_: (0,) * nd, pipeline_mode=pl.Buffered(1))


def _block_ones(n, block):
    r = lax.broadcasted_iota(jnp.int32, (n, n), 0) // block
    c = lax.broadcasted_iota(jnp.int32, (n, n), 1) // block
    return (r == c).astype(BF16)


def _group_sumsq(v, block):
    ones = _block_ones(MXU_DIM, block)
    sq = (v * v).astype(BF16)
    parts = [_dot(sq[:, c:c + MXU_DIM], ones) for c in range(0, v.shape[1], MXU_DIM)]
    return parts[0] if len(parts) == 1 else jnp.concatenate(parts, axis=1)


def _causal_conv(cur, prev, w):
    width = w.shape[0]
    sub = lax.broadcasted_iota(jnp.int32, prev.shape, 0)
    out = w[width - 1:width, :] * cur
    for s in range(1, width):
        rolled = pltpu.roll(cur, s, axis=0)
        head = jnp.where(sub < s, pltpu.roll(prev, s, axis=0), rolled[0:SUBLANES])
        shifted = jnp.concatenate([head, rolled[SUBLANES:]], axis=0)
        out = out + w[width - 1 - s:width - s, :] * shifted
    return out


def _params(sem):
    return pltpu.CompilerParams(dimension_semantics=sem, vmem_limit_bytes=VMEM_LIMIT_BYTES)


def _ffn_kernel(x_ref, gain_ref, wgu_ref, wd_ref, o_ref):
    xf = x_ref[...]
    xn = _rms_norm_rows(xf, gain_ref[...]).astype(BF16)
    acc = xf
    for lo, hi in zip(FFN_SPLITS[:-1], FFN_SPLITS[1:]):
        g = _dot(xn, wgu_ref[:, lo:hi])
        u = _dot(xn, wgu_ref[:, D_FF + lo:D_FF + hi])
        act = (_silu(g) * u).astype(BF16)
        acc = acc + _dot(act, wd_ref[lo:hi, :])
    o_ref[...] = acc


def _ffn(x2d, gain, w_gate_up, w_down):
    n_rows = x2d.shape[0]
    wgu = w_gate_up.astype(BF16)
    wd = w_down.astype(BF16)
    return pl.pallas_call(
        _ffn_kernel,
        grid=(n_rows // ROW_TILE,),
        in_specs=[
            pl.BlockSpec((ROW_TILE, D_MODEL), lambda i: (i, 0)),
            _const_spec((1, D_MODEL)),
            _const_spec((D_MODEL, 2 * D_FF)),
            _const_spec((D_FF, D_MODEL)),
        ],
        out_specs=pl.BlockSpec((ROW_TILE, D_MODEL), lambda i: (i, 0)),
        out_shape=jax.ShapeDtypeStruct((n_rows, D_MODEL), F32),
        compiler_params=_params(("parallel",)),
        name="ffn",
    )(x2d, gain.reshape(1, D_MODEL), wgu, wd)


QK_COLS = Q_W + 2 * KV_W


def _inproj0_kernel(x_ref, gain_ref, w_ref, qkgain_ref, cos_ref, sin_ref,
                    q_ref, k_ref, v_ref, gb_ref, cu_ref):
    xn = _rms_norm_rows(x_ref[...], gain_ref[...]).astype(BF16)
    qk = _dot(xn, w_ref[:, :QK_COLS])
    ss = _group_sumsq(qk, HEAD_DIM)
    qk = qk * lax.rsqrt(ss * (1.0 / HEAD_DIM) + EPS) * qkgain_ref[...]
    reps = QK_COLS // LANES
    cos = jnp.concatenate([cos_ref[...]] * reps, axis=1)
    sin = jnp.concatenate([sin_ref[...]] * reps, axis=1)
    half = HEAD_DIM // 2
    lane = lax.broadcasted_iota(jnp.int32, qk.shape, 1)
    partner = jnp.where((lane % HEAD_DIM) < half,
                        pltpu.roll(qk, QK_COLS - half, axis=1),
                        pltpu.roll(qk, half, axis=1))
    qk = qk * cos + partner * sin
    q_ref[...] = (qk[:, :Q_W] * (HEAD_DIM ** -0.5)).astype(BF16)
    k_ref[...] = qk[:, Q_W:].astype(BF16)
    c0 = QK_COLS
    v_ref[...] = _dot(xn, w_ref[:, c0:c0 + 2 * KV_W]).astype(BF16)
    c0 += 2 * KV_W
    gb_ref[...] = _dot(xn, w_ref[:, c0:c0 + CONV_CH]).astype(BF16)
    c0 += CONV_CH
    gc = _dot(xn, w_ref[:, c0:c0 + CONV_CH])
    xin = _dot(xn, w_ref[:, c0 + CONV_CH:c0 + 2 * CONV_CH])
    cu_ref[...] = (gc * xin).astype(BF16)


def _dup_heads(w, n_heads, dim):
    w3 = w.reshape(w.shape[0], n_heads, 1, dim)
    return jnp.concatenate([w3, w3], axis=2).reshape(w.shape[0], 2 * n_heads * dim)


def _inproj0(x2d, seq, gain, w_in, q_gain, k_gain):
    n_rows = x2d.shape[0]
    wq = w_in[:, :Q_W]
    wk = _dup_heads(w_in[:, Q_W:Q_W + KV_W], ATTN_KV_HEADS, HEAD_DIM)
    wv = _dup_heads(w_in[:, Q_W + KV_W:Q_W + 2 * KV_W], ATTN_KV_HEADS, HEAD_DIM)
    w = jnp.concatenate([wq, wk, wv, w_in[:, Q_W + 2 * KV_W:]], axis=1).astype(BF16)
    n_cols = w.shape[1]
    qkgain = jnp.concatenate([jnp.tile(q_gain, ATTN_HEADS), jnp.tile(k_gain, 2 * ATTN_KV_HEADS)]
                             ).reshape(1, QK_COLS).astype(F32)
    inv_freq = ROPE_THETA ** (-jnp.arange(0, HEAD_DIM, 2, dtype=F32) / HEAD_DIM)
    ang = jnp.arange(seq, dtype=F32)[:, None] * inv_freq[None, :]
    cos, sin = jnp.cos(ang), jnp.sin(ang)
    cos_t = jnp.tile(jnp.concatenate([cos, cos], axis=1), (1, LANES // HEAD_DIM))
    sin_t = jnp.tile(jnp.concatenate([-sin, sin], axis=1), (1, LANES // HEAD_DIM))
    tiles_per_seq = seq // ROW_TILE
    row = lambda i: (i, 0)
    pos = lambda i: (i % tiles_per_seq, 0)
    return pl.pallas_call(
        _inproj0_kernel,
        grid=(n_rows // ROW_TILE,),
        in_specs=[
            pl.BlockSpec((ROW_TILE, D_MODEL), row),
            _const_spec((1, D_MODEL)),
            _const_spec((D_MODEL, n_cols)),
            _const_spec((1, QK_COLS)),
            pl.BlockSpec((ROW_TILE, LANES), pos),
            pl.BlockSpec((ROW_TILE, LANES), pos),
        ],
        out_specs=[
            pl.BlockSpec((ROW_TILE, Q_W), row),
            pl.BlockSpec((ROW_TILE, 2 * KV_W), row),
            pl.BlockSpec((ROW_TILE, 2 * KV_W), row),
            pl.BlockSpec((ROW_TILE, CONV_CH), row),
            pl.BlockSpec((ROW_TILE, CONV_CH), row),
        ],
        out_shape=[
            jax.ShapeDtypeStruct((n_rows, Q_W), BF16),
            jax.ShapeDtypeStruct((n_rows, 2 * KV_W), BF16),
            jax.ShapeDtypeStruct((n_rows, 2 * KV_W), BF16),
            jax.ShapeDtypeStruct((n_rows, CONV_CH), BF16),
            jax.ShapeDtypeStruct((n_rows, CONV_CH), BF16),
        ],
        compiler_params=_params(("parallel",)),
        name="inproj0",
    )(x2d, gain.reshape(1, D_MODEL), w, qkgain, cos_t, sin_t)


CONV0_WIDTH = 3


def _mixer0_kernel(sinks_ref, q_ref, k_ref, v_ref, kp_ref, vp_ref, gb_ref, cu_ref, cup_ref,
                   x_ref, convw_ref, wout_ref, o_ref, y_ref):
    first = pl.program_id(1) == 0
    n_blocks = MIX_ROWS // ATTN_BLOCK
    grp = ATTN_HEADS // ATTN_KV_HEADS
    r = lax.broadcasted_iota(jnp.int32, (ATTN_BLOCK, 2 * ATTN_BLOCK), 0)
    c = lax.broadcasted_iota(jnp.int32, (ATTN_BLOCK, 2 * ATTN_BLOCK), 1)
    band = (c - r >= 1) & (c - r <= ATTN_BLOCK)
    lane = lax.broadcasted_iota(jnp.int32, (ATTN_BLOCK, LANES), 1)
    low_half = lane < HEAD_DIM
    for j in range(n_blocks):
        rows = slice(j * ATTN_BLOCK, (j + 1) * ATTN_BLOCK)
        if j == 0:
            k_prev, v_prev = kp_ref[...], vp_ref[...]
            valid = band & ((c >= ATTN_BLOCK) | jnp.logical_not(first))
        else:
            prev = slice((j - 1) * ATTN_BLOCK, j * ATTN_BLOCK)
            k_prev, v_prev = k_ref[prev, :], v_ref[prev, :]
            valid = band
        k2 = jnp.concatenate([k_prev, k_ref[rows, :]], axis=0)
        v2 = jnp.concatenate([v_prev, v_ref[rows, :]], axis=0)
        for pair in range(ATTN_HEADS // 2):
            kv = (2 * pair) // grp
            kk = k2[:, kv * LANES:(kv + 1) * LANES]
            vv = v2[:, kv * LANES:(kv + 1) * LANES]
            q2 = q_ref[rows, pair * LANES:(pair + 1) * LANES]
            outs = []
            for sub in range(2):
                head = 2 * pair + sub
                keep = low_half if sub == 0 else jnp.logical_not(low_half)
                qh = jnp.where(keep, q2, jnp.zeros_like(q2))
                s = _dot_nt(qh, kk)
                s = jnp.where(valid, s, NEG_BIG)
                sink = sinks_ref[head]
                m = jnp.maximum(jnp.max(s, axis=-1, keepdims=True), sink)
                p = jnp.exp(s - m)
                denom = jnp.sum(p, axis=-1, keepdims=True) + jnp.exp(sink - m)
                o = _dot(p.astype(BF16), vv)
                outs.append(o * (1.0 / denom))
            y_ref[rows, pair * LANES:(pair + 1) * LANES] = jnp.where(
                low_half, outs[0], outs[1]).astype(BF16)
    prev_rows = cup_ref[...].astype(F32)
    prev_rows = jnp.where(first, jnp.zeros_like(prev_rows), prev_rows)
    conv = _causal_conv(cu_ref[...].astype(F32), prev_rows, convw_ref[...])
    y_ref[:, Q_W:] = (gb_ref[...].astype(F32) * conv).astype(BF16)
    o_ref[...] = x_ref[...] + _dot(y_ref[...], wout_ref[...])


def _mixer0(x2d, bsz, seq, q, k, v, gb, cu, sinks, conv_w, w_out):
    n_rows = x2d.shape[0]
    tiles = seq // MIX_ROWS
    blk_per_tile = MIX_ROWS // ATTN_BLOCK
    sub_per_tile = MIX_ROWS // SUBLANES
    row = lambda b, i: (b * tiles + i, 0)
    prev_blk = lambda b, i: (jnp.maximum((b * tiles + i) * blk_per_tile - 1, 0), 0)
    prev_sub = lambda b, i: (jnp.maximum((b * tiles + i) * sub_per_tile - 1, 0), 0)
    return pl.pallas_call(
        _mixer0_kernel,
        grid=(bsz, tiles),
        in_specs=[
            pl.BlockSpec(memory_space=pltpu.SMEM),
            pl.BlockSpec((MIX_ROWS, Q_W), row),
            pl.BlockSpec((MIX_ROWS, 2 * KV_W), row),
            pl.BlockSpec((MIX_ROWS, 2 * KV_W), row),
            pl.BlockSpec((ATTN_BLOCK, 2 * KV_W), prev_blk),
            pl.BlockSpec((ATTN_BLOCK, 2 * KV_W), prev_blk),
            pl.BlockSpec((MIX_ROWS, CONV_CH), row),
            pl.BlockSpec((MIX_ROWS, CONV_CH), row),
            pl.BlockSpec((SUBLANES, CONV_CH), prev_sub),
            pl.BlockSpec((MIX_ROWS, D_MODEL), row),
            _const_spec((CONV0_WIDTH, CONV_CH)),
            _const_spec((D_MODEL, D_MODEL)),
        ],
        out_specs=pl.BlockSpec((MIX_ROWS, D_MODEL), row),
        out_shape=jax.ShapeDtypeStruct((n_rows, D_MODEL), F32),
        scratch_shapes=[
            pltpu.VMEM((MIX_ROWS, D_MODEL), BF16),
        ],
        compiler_params=_params(("parallel", "arbitrary")),
        name="mixer0",
    )(sinks.astype(F32), q, k, v, k, v, gb, cu, cu, x2d, conv_w.astype(F32), w_out.astype(BF16))


GATE_LANES = LANES


def _inproj1_kernel(x_ref, gain_ref, w_ref, wab_ref, convw_ref, qkv_ref, z_ref, ab_ref, halo_ref):
    @pl.when(pl.program_id(1) == 0)
    def _():
        halo_ref[...] = jnp.zeros_like(halo_ref)

    xn = _rms_norm_rows(x_ref[...], gain_ref[...]).astype(BF16)
    ones = _block_ones(MXU_DIM, DN_HEAD_DIM)
    for c in range(3):
        cols = slice(c * DN_W, (c + 1) * DN_W)
        proj = _dot(xn, w_ref[:, cols])
        conv = _causal_conv(proj, halo_ref[:, cols], convw_ref[:, cols])
        halo_ref[:, cols] = proj[ROW_TILE - SUBLANES:, :]
        act = _silu(conv)
        if c < 2:
            scale = float(DN_HEAD_DIM) if c == 0 else 1.0
            summer = ones * scale if c == 0 else ones
            sq = (act * act).astype(BF16)
            ss = jnp.concatenate([_dot(sq[:, c0:c0 + MXU_DIM], summer) for c0 in range(0, DN_W, MXU_DIM)],
                                 axis=1)
            act = act * lax.rsqrt(ss + EPS * scale)
        qkv_ref[:, cols] = act.astype(BF16)
    z_ref[...] = _dot(xn, w_ref[:, 3 * DN_W:]).astype(BF16)
    ab_ref[...] = _dot(xn, wab_ref[...])


def _inproj1(x2d, bsz, seq, gain, w_in, conv_w):
    n_rows = x2d.shape[0]
    w = w_in[:, :4 * DN_W].astype(BF16)
    wab = jnp.pad(w_in[:, 4 * DN_W:], ((0, 0), (0, GATE_LANES - 2 * DN_HEADS))).astype(BF16)
    tiles = seq // ROW_TILE
    row = lambda b, i: (b * tiles + i, 0)
    return pl.pallas_call(
        _inproj1_kernel,
        grid=(bsz, tiles),
        in_specs=[
            pl.BlockSpec((ROW_TILE, D_MODEL), row),
            _const_spec((1, D_MODEL)),
            _const_spec((D_MODEL, 4 * DN_W)),
            _const_spec((D_MODEL, GATE_LANES)),
            _const_spec((DN_CONV_WIDTH, 3 * DN_W)),
        ],
        out_specs=[
            pl.BlockSpec((ROW_TILE, 3 * DN_W), row),
            pl.BlockSpec((ROW_TILE, DN_W), row),
            pl.BlockSpec((ROW_TILE, GATE_LANES), row),
        ],
        out_shape=[
            jax.ShapeDtypeStruct((n_rows, 3 * DN_W), BF16),
            jax.ShapeDtypeStruct((n_rows, DN_W), BF16),
            jax.ShapeDtypeStruct((n_rows, GATE_LANES), F32),
        ],
        scratch_shapes=[
            pltpu.VMEM((SUBLANES, 3 * DN_W), F32),
        ],
        compiler_params=_params(("parallel", "arbitrary")),
        name="inproj1",
    )(x2d, gain.reshape(1, D_MODEL), w, wab, conv_w.astype(F32))


GDN_ROWS = MIX_ROWS
GDN_CHUNKS = GDN_ROWS // DN_CHUNK
GDN_GROUP = 2
SPLIT_TERMS = 3


def _split_bf16(v, terms):
    parts, rem = [], v
    for _ in range(terms):
        p = rem.astype(BF16)
        parts.append(p)
        rem = rem - p.astype(F32)
    return parts


def _softplus(v):
    return jnp.maximum(v, 0.0) + jnp.log(1.0 + jnp.exp(-jnp.abs(v)))


def _gdn_kernel(qkv_ref, z_ref, ab_ref, abt_ref, x_ref, gcol_ref, grow_ref, ogain_ref,
                wout_ref, o_ref, state_ref, gate_ref, o_acc_ref):
    @pl.when(pl.program_id(1) == 0)
    def _():
        state_ref[...] = jnp.zeros_like(state_ref)

    gcoef = gcol_ref[...]
    ab = ab_ref[...]
    gate_ref[:, 0:LANES] = gcoef[0:1, :] * _softplus(ab + gcoef[1:2, :])
    gate_ref[:, LANES:2 * LANES] = _sigmoid(ab)
    rcoef = grow_ref[...]

    ci = lax.broadcasted_iota(jnp.int32, (DN_CHUNK, DN_CHUNK), 0)
    cj = lax.broadcasted_iota(jnp.int32, (DN_CHUNK, DN_CHUNK), 1)
    lower = ci >= cj
    strict = ci > cj
    tril = lower.astype(BF16)
    triu = (ci <= cj).astype(BF16)

    eye = jnp.where(ci == cj, 1.0, 0.0)
    heads = range(DN_HEADS)

    def group_body(it, carry):
        chunks = [it * GDN_GROUP + j for j in range(GDN_GROUP)]
        rows = [pl.ds(pl.multiple_of(c * DN_CHUNK, DN_CHUNK), DN_CHUNK) for c in chunks]
        items = [(j, h) for j in range(GDN_GROUP) for h in heads]
        gc_col = [sum(_dot(tril, p) for p in _split_bf16(gate_ref[r, 0:LANES], SPLIT_TERMS))
                  for r in rows]
        g_row = [rcoef[:, 0:1] * _softplus(abt_ref[c] + rcoef[:, 1:2]) for c in chunks]
        gc_row = [sum(_dot(p, triu) for p in _split_bf16(g, SPLIT_TERMS)) for g in g_row]
        beta_c = [gate_ref[r, LANES:2 * LANES] for r in rows]

        def head_cols(j, h, group):
            return qkv_ref[rows[j], group * DN_W + h * LANES:group * DN_W + (h + 1) * LANES]

        q16 = [head_cols(j, h, 0) for j, h in items]
        k16 = [head_cols(j, h, 1) for j, h in items]
        k = [t.astype(F32) for t in k16]
        v = [head_cols(j, h, 2).astype(F32) for j, h in items]
        beta = [beta_c[j][:, h:h + 1] for j, h in items]
        gcc = [gc_col[j][:, DN_HEADS + h:DN_HEADS + h + 1] for j, h in items]
        gcr = [gc_row[j][DN_HEADS + h:DN_HEADS + h + 1, :] for j, h in items]
        gc_last = [g[DN_CHUNK - 1:DN_CHUNK, :] for g in gcc]
        n = range(len(items))
        eg = [jnp.exp(g) for g in gcc]
        gamma = [jnp.exp(jnp.where(lower, gcc[i] - gcr[i], -jnp.inf)) for i in n]
        kb = [k[i] * beta[i] for i in n]
        s1 = [_dot_nt(jnp.concatenate([kb[i].astype(BF16), q16[i]], axis=0), k16[i]) for i in n]
        attn = [(s1[i][DN_CHUNK:] * gamma[i]).astype(BF16) for i in n]
        b = [jnp.where(strict, -(s1[i][:DN_CHUNK] * gamma[i]), 0.0).astype(BF16) for i in n]
        p = [eye + b[i].astype(F32) for i in n]
        b = [_dot(b[i], b[i]).astype(BF16) for i in n]
        power = 2
        while power < DN_CHUNK:
            last = 2 * power >= DN_CHUNK
            lhs = [p[i].astype(BF16) if last else jnp.concatenate([b[i], p[i].astype(BF16)], axis=0)
                   for i in n]
            prod = [_dot(lhs[i], b[i]) for i in n]
            if last:
                p = [p[i] + prod[i] for i in n]
            else:
                b = [prod[i][:DN_CHUNK].astype(BF16) for i in n]
                p = [p[i] + prod[i][DN_CHUNK:] for i in n]
            power *= 2
        rhs = [jnp.concatenate([v[i] * beta[i], kb[i] * eg[i]], axis=1).astype(BF16) for i in n]
        uw = [_dot(p[i].astype(BF16), rhs[i]).astype(BF16) for i in n]
        kd = [(k[i] * jnp.exp(gc_last[i] - gcc[i])).astype(BF16) for i in n]
        nm = [_dot_tn(kd[i], uw[i]) for i in n]
        oa = [_dot(attn[i], uw[i]) for i in n]
        lhs = [jnp.concatenate([nm[i][:, LANES:], q16[i].astype(F32) * eg[i] - oa[i][:, LANES:]],
                               axis=0).astype(BF16) for i in n]
        decay = [jnp.exp(g) for g in gc_last]
        state = [state_ref[h] for h in heads]
        for j in range(GDN_GROUP):
            idx = [j * DN_HEADS + h for h in heads]
            res = [_dot(lhs[i], state[h].astype(BF16)) for h, i in zip(heads, idx)]
            state = [decay[i] * state[h] - res[h][:DN_HEAD_DIM] + nm[i][:, :LANES] for h, i in zip(heads, idx)]
            for h, i in zip(heads, idx):
                o_acc_ref[rows[j], h * LANES:(h + 1) * LANES] = res[h][DN_HEAD_DIM:] + oa[i][:, :LANES]
        for h in heads:
            state_ref[h] = state[h]
        return carry

    lax.fori_loop(0, GDN_CHUNKS // GDN_GROUP, group_body, 0)

    o = o_acc_ref[...]
    ones = _block_ones(MXU_DIM, DN_HEAD_DIM)
    sq = (o * o).astype(BF16)
    ss = jnp.concatenate([_dot(sq[:, c0:c0 + MXU_DIM], ones) for c0 in range(0, DN_W, MXU_DIM)], axis=1)
    y = o * lax.rsqrt(ss * (1.0 / DN_HEAD_DIM) + EPS) * ogain_ref[...] * _silu(z_ref[...].astype(F32))
    o_ref[...] = x_ref[...] + _dot(y.astype(BF16), wout_ref[...])


def _layer1_mixer(x2d, bsz, seq, norm_gain, w_in, conv_w, a_log, dt_bias, o_gain, w_out):
    n_rows = x2d.shape[0]
    qkv, z, ab = _inproj1(x2d, bsz, seq, norm_gain, w_in, conv_w)
    n_chunks = n_rows // DN_CHUNK
    abt = ab[:, :2 * DN_HEADS].reshape(n_chunks, DN_CHUNK, 2 * DN_HEADS).transpose(0, 2, 1)
    neg_a = -jnp.exp(a_log.astype(F32))
    pad_lo = jnp.zeros((DN_HEADS,), F32)
    gcol = jnp.zeros((2, LANES), F32)
    gcol = gcol.at[0, DN_HEADS:2 * DN_HEADS].set(neg_a).at[1, DN_HEADS:2 * DN_HEADS].set(dt_bias.astype(F32))
    grow = jnp.stack([jnp.concatenate([pad_lo, neg_a]), jnp.concatenate([pad_lo, dt_bias.astype(F32)])], axis=1)
    ogain = jnp.tile(o_gain.astype(F32), DN_HEADS).reshape(1, DN_W)
    tiles = seq // GDN_ROWS
    row = lambda b, i: (b * tiles + i, 0)
    return pl.pallas_call(
        _gdn_kernel,
        grid=(bsz, tiles),
        in_specs=[
            pl.BlockSpec((GDN_ROWS, 3 * DN_W), row),
            pl.BlockSpec((GDN_ROWS, DN_W), row),
            pl.BlockSpec((GDN_ROWS, GATE_LANES), row),
            pl.BlockSpec((GDN_CHUNKS, 2 * DN_HEADS, DN_CHUNK), lambda b, i: (b * tiles + i, 0, 0)),
            pl.BlockSpec((GDN_ROWS, D_MODEL), row),
            _const_spec((2, LANES)),
            _const_spec((2 * DN_HEADS, 2)),
            _const_spec((1, DN_W)),
            _const_spec((DN_W, D_MODEL)),
        ],
        out_specs=pl.BlockSpec((GDN_ROWS, D_MODEL), row),
        out_shape=jax.ShapeDtypeStruct((n_rows, D_MODEL), F32),
        scratch_shapes=[
            pltpu.VMEM((DN_HEADS, DN_HEAD_DIM, DN_HEAD_DIM), F32),
            pltpu.VMEM((GDN_ROWS, 2 * LANES), F32),
            pltpu.VMEM((GDN_ROWS, DN_W), F32),
        ],
        compiler_params=_params(("parallel", "arbitrary")),
        name="gdn",
    )(qkv, z, ab, abt, x2d, gcol, grow, ogain, w_out.astype(BF16))


def kernel(x, even_norm, even_w_in, even_q_gain, even_k_gain, even_sinks, even_conv_w, even_w_out, odd_norm, odd_w_in, odd_conv_w, odd_a_log, odd_dt_bias, odd_o_gain, odd_w_out, ffn_norm, ffn_w_gate_up, ffn_w_down):
    bsz, seq, _ = x.shape
    x2d = x.reshape(bsz * seq, D_MODEL)
    q, k, v, gb, cu = _inproj0(x2d, seq, even_norm[0], even_w_in[0], even_q_gain[0], even_k_gain[0])
    x2d = _mixer0(x2d, bsz, seq, q, k, v, gb, cu, even_sinks[0], even_conv_w[0], even_w_out[0])
    x2d = _ffn(x2d, ffn_norm[0], ffn_w_gate_up[0], ffn_w_down[0])
    x2d = _layer1_mixer(x2d, bsz, seq, odd_norm[0], odd_w_in[0], odd_conv_w[0], odd_a_log[0],
                        odd_dt_bias[0], odd_o_gain[0], odd_w_out[0])
    x2d = _ffn(x2d, ffn_norm[1], ffn_w_gate_up[1], ffn_w_down[1])
    return x2d.reshape(bsz, seq, D_MODEL)
```

```python
import functools

import jax
import jax.numpy as jnp
from jax import lax
from jax.experimental import pallas as pl
from jax.experimental.pallas import tpu as pltpu

D_MODEL = 1024
D_FF = 2816
EPS = 1e-6

HEAD_DIM = 64
ATTN_HEADS = 8
ATTN_KV_HEADS = 2
ATTN_BLOCK = 128
ROPE_THETA = 10000.0
CONV_CH = 512
Q_W = ATTN_HEADS * HEAD_DIM
KV_W = ATTN_KV_HEADS * HEAD_DIM
DN_HEAD_DIM = 128
DN_HEADS = 8
DN_W = DN_HEADS * DN_HEAD_DIM
DN_CONV_WIDTH = 4
DN_CHUNK = 64

LANES = 128
SUBLANES = 8
MXU_DIM = 256
VMEM_LIMIT_BYTES = 56 * 1024 * 1024

ROW_TILE = 512
FFN_ROWS = 512
FFN_SPLITS = (0, 1536, D_FF)
MIX_ROWS = 512
NEG_BIG = -1e30

BF16 = jnp.bfloat16
F32 = jnp.float32


def _rms_norm_rows(xf, gain):
    ms = jnp.mean(xf * xf, axis=-1, keepdims=True)
    return xf * lax.rsqrt(ms + EPS) * gain


def _sigmoid(v):
    return 1.0 / (1.0 + jnp.exp(-v))


def _silu(v):
    return v * _sigmoid(v)


def _dot(a, b):
    return jnp.dot(a, b, preferred_element_type=F32)


def _dot_nt(a, b):
    return lax.dot_general(a, b, (((1,), (1,)), ((), ())), preferred_element_type=F32)


def _dot_tn(a, b):
    return lax.dot_general(a, b, (((0,), (0,)), ((), ())), preferred_element_type=F32)


def _const_spec(shape):
    nd = len(shape)
    return pl.BlockSpec(shape, lambda *_: (0,) * nd, pipeline_mode=pl.Buffered(1))


def _block_ones(n, block):
    r = lax.broadcasted_iota(jnp.int32, (n, n), 0) // block
    c = lax.broadcasted_iota(jnp.int32, (n, n), 1) // block
    return (r == c).astype(BF16)


def _group_sumsq(v, block, scale):
    ones = _block_ones(MXU_DIM, block) * scale
    sq = (v * v).astype(BF16)
    parts = [_dot(sq[:, c:c + MXU_DIM], ones) for c in range(0, v.shape[1], MXU_DIM)]
    return parts[0] if len(parts) == 1 else jnp.concatenate(parts, axis=1)


def _causal_conv(cur, prev, w):
    width = w.shape[0]
    sub = lax.broadcasted_iota(jnp.int32, prev.shape, 0)
    out = w[width - 1:width, :] * cur
    for s in range(1, width):
        rolled = pltpu.roll(cur, s, axis=0)
        head = jnp.where(sub < s, pltpu.roll(prev, s, axis=0), rolled[0:SUBLANES])
        shifted = jnp.concatenate([head, rolled[SUBLANES:]], axis=0)
        out = out + w[width - 1 - s:width - s, :] * shifted
    return out


def _params(sem):
    return pltpu.CompilerParams(dimension_semantics=sem, vmem_limit_bytes=VMEM_LIMIT_BYTES)


def _ffn_rows(xf, gain_ref, wgu_ref, wd_ref):
    xn = _rms_norm_rows(xf, gain_ref[...]).astype(BF16)
    acc = xf
    for lo, hi in zip(FFN_SPLITS[:-1], FFN_SPLITS[1:]):
        g = _dot(xn, wgu_ref[:, lo:hi])
        u = _dot(xn, wgu_ref[:, D_FF + lo:D_FF + hi])
        act = (_silu(g) * u).astype(BF16)
        acc = acc + _dot(act, wd_ref[lo:hi, :])
    return acc


def _ffn_kernel(x_ref, gain_ref, wgu_ref, wd_ref, o_ref):
    o_ref[...] = _ffn_rows(x_ref[...], gain_ref, wgu_ref, wd_ref)


def _ffn(x2d, gain, w_gate_up, w_down):
    n_rows = x2d.shape[0]
    wgu = w_gate_up.astype(BF16)
    wd = w_down.astype(BF16)
    return pl.pallas_call(
        _ffn_kernel,
        grid=(n_rows // FFN_ROWS,),
        in_specs=[
            pl.BlockSpec((FFN_ROWS, D_MODEL), lambda i: (i, 0)),
            _const_spec((1, D_MODEL)),
            _const_spec((D_MODEL, 2 * D_FF)),
            _const_spec((D_FF, D_MODEL)),
        ],
        out_specs=pl.BlockSpec((FFN_ROWS, D_MODEL), lambda i: (i, 0)),
        out_shape=jax.ShapeDtypeStruct((n_rows, D_MODEL), F32),
        compiler_params=_params(("parallel",)),
        name="ffn",
    )(x2d, gain.reshape(1, D_MODEL), wgu, wd)


QK_COLS = Q_W + 2 * KV_W


def _inproj0_kernel(x_ref, gain_ref, w_ref, qkgain_ref, cos_ref, sin_ref,
                    q_ref, k_ref, v_ref, gb_ref, cu_ref):
    xn = _rms_norm_rows(x_ref[...], gain_ref[...]).astype(BF16)
    qk = _dot(xn, w_ref[:, :QK_COLS])
    c0 = QK_COLS
    v_ref[...] = _dot(xn, w_ref[:, c0:c0 + 2 * KV_W]).astype(BF16)
    c0 += 2 * KV_W
    ms = _group_sumsq(qk, HEAD_DIM, 1.0 / HEAD_DIM)
    gb_ref[...] = _dot(xn, w_ref[:, c0:c0 + CONV_CH]).astype(BF16)
    c0 += CONV_CH
    qk = qk * lax.rsqrt(ms + EPS) * qkgain_ref[...]
    gc = _dot(xn, w_ref[:, c0:c0 + CONV_CH])
    xin = _dot(xn, w_ref[:, c0 + CONV_CH:c0 + 2 * CONV_CH])
    cu_ref[...] = (gc * xin).astype(BF16)
    reps = QK_COLS // LANES
    cos = jnp.concatenate([cos_ref[...]] * reps, axis=1)
    sin = jnp.concatenate([sin_ref[...]] * reps, axis=1)
    half = HEAD_DIM // 2
    lane = lax.broadcasted_iota(jnp.int32, qk.shape, 1)
    partner = jnp.where((lane % HEAD_DIM) < half,
                        pltpu.roll(qk, QK_COLS - half, axis=1),
                        pltpu.roll(qk, half, axis=1))
    qk = qk * cos + partner * sin
    q_ref[...] = qk[:, :Q_W].astype(BF16)
    k_ref[...] = qk[:, Q_W:].astype(BF16)


def _dup_heads(w, n_heads, dim):
    w3 = w.reshape(w.shape[0], n_heads, 1, dim)
    return jnp.concatenate([w3, w3], axis=2).reshape(w.shape[0], 2 * n_heads * dim)


def _inproj0(x2d, seq, gain, w_in, q_gain, k_gain):
    n_rows = x2d.shape[0]
    wq = w_in[:, :Q_W]
    wk = _dup_heads(w_in[:, Q_W:Q_W + KV_W], ATTN_KV_HEADS, HEAD_DIM)
    wv = _dup_heads(w_in[:, Q_W + KV_W:Q_W + 2 * KV_W], ATTN_KV_HEADS, HEAD_DIM)
    w = jnp.concatenate([wq, wk, wv, w_in[:, Q_W + 2 * KV_W:]], axis=1).astype(BF16)
    n_cols = w.shape[1]
    qkgain = jnp.concatenate([jnp.tile(q_gain * (HEAD_DIM ** -0.5), ATTN_HEADS),
                              jnp.tile(k_gain, 2 * ATTN_KV_HEADS)]).reshape(1, QK_COLS).astype(F32)
    inv_freq = ROPE_THETA ** (-jnp.arange(0, HEAD_DIM, 2, dtype=F32) / HEAD_DIM)
    ang = jnp.arange(seq, dtype=F32)[:, None] * inv_freq[None, :]
    cos, sin = jnp.cos(ang), jnp.sin(ang)
    cos_t = jnp.tile(jnp.concatenate([cos, cos], axis=1), (1, LANES // HEAD_DIM))
    sin_t = jnp.tile(jnp.concatenate([-sin, sin], axis=1), (1, LANES // HEAD_DIM))
    tiles_per_seq = seq // ROW_TILE
    row = lambda i: (i, 0)
    pos = lambda i: (i % tiles_per_seq, 0)
    return pl.pallas_call(
        _inproj0_kernel,
        grid=(n_rows // ROW_TILE,),
        in_specs=[
            pl.BlockSpec((ROW_TILE, D_MODEL), row),
            _const_spec((1, D_MODEL)),
            _const_spec((D_MODEL, n_cols)),
            _const_spec((1, QK_COLS)),
            pl.BlockSpec((ROW_TILE, LANES), pos),
            pl.BlockSpec((ROW_TILE, LANES), pos),
        ],
        out_specs=[
            pl.BlockSpec((ROW_TILE, Q_W), row),
            pl.BlockSpec((ROW_TILE, 2 * KV_W), row),
            pl.BlockSpec((ROW_TILE, 2 * KV_W), row),
            pl.BlockSpec((ROW_TILE, CONV_CH), row),
            pl.BlockSpec((ROW_TILE, CONV_CH), row),
        ],
        out_shape=[
            jax.ShapeDtypeStruct((n_rows, Q_W), BF16),
            jax.ShapeDtypeStruct((n_rows, 2 * KV_W), BF16),
            jax.ShapeDtypeStruct((n_rows, 2 * KV_W), BF16),
            jax.ShapeDtypeStruct((n_rows, CONV_CH), BF16),
            jax.ShapeDtypeStruct((n_rows, CONV_CH), BF16),
        ],
        compiler_params=_params(("parallel",)),
        name="inproj0",
    )(x2d, gain.reshape(1, D_MODEL), w, qkgain, cos_t, sin_t)


CONV0_WIDTH = 3


def _mixer0_kernel(sinks_ref, q_ref, k_ref, v_ref, kp_ref, vp_ref, gb_ref, cu_ref, cup_ref,
                   x_ref, convw_ref, wout_ref, o_ref, y_ref):
    first = pl.program_id(1) == 0
    n_blocks = MIX_ROWS // ATTN_BLOCK
    grp = ATTN_HEADS // ATTN_KV_HEADS
    r = lax.broadcasted_iota(jnp.int32, (ATTN_BLOCK, 2 * ATTN_BLOCK), 0)
    c = lax.broadcasted_iota(jnp.int32, (ATTN_BLOCK, 2 * ATTN_BLOCK), 1)
    band = (c - r >= 1) & (c - r <= ATTN_BLOCK)
    lane = lax.broadcasted_iota(jnp.int32, (ATTN_BLOCK, LANES), 1)
    low_half = lane < HEAD_DIM
    block_rows = [slice(j * ATTN_BLOCK, (j + 1) * ATTN_BLOCK) for j in range(n_blocks)]

    def keys_values(ref, prev_ref, j, kv):
        before = prev_ref[:, kv * LANES:(kv + 1) * LANES] if j == 0 else ref[block_rows[j - 1], kv * LANES:(kv + 1) * LANES]
        return jnp.concatenate([before, ref[block_rows[j], kv * LANES:(kv + 1) * LANES]], axis=0)

    def head_queries(j, head):
        q2 = q_ref[block_rows[j], (head // 2) * LANES:(head // 2 + 1) * LANES]
        keep = low_half if head % 2 == 0 else jnp.logical_not(low_half)
        return jnp.where(keep, q2, jnp.zeros_like(q2))

    first_valid = band & ((c >= ATTN_BLOCK) | jnp.logical_not(first))
    heads = range(ATTN_HEADS)
    for j in range(n_blocks):
        valid = first_valid if j == 0 else band
        kk = [keys_values(k_ref, kp_ref, j, kv) for kv in range(ATTN_KV_HEADS)]
        vv = [keys_values(v_ref, vp_ref, j, kv) for kv in range(ATTN_KV_HEADS)]
        s = [jnp.where(valid, _dot_nt(head_queries(j, h), kk[h // grp]), NEG_BIG) for h in heads]
        m = [jnp.maximum(jnp.max(s[h], axis=-1, keepdims=True), sinks_ref[h]) for h in heads]
        p = [jnp.exp(s[h] - m[h]) for h in heads]
        denom = [jnp.sum(p[h], axis=-1, keepdims=True) + jnp.exp(sinks_ref[h] - m[h]) for h in heads]
        o = [_dot(p[h].astype(BF16), vv[h // grp]) * (1.0 / denom[h]) for h in heads]
        for pair in range(ATTN_HEADS // 2):
            y_ref[block_rows[j], pair * LANES:(pair + 1) * LANES] = jnp.where(
                low_half, o[2 * pair], o[2 * pair + 1]).astype(BF16)
    prev_rows = cup_ref[...].astype(F32)
    prev_rows = jnp.where(first, jnp.zeros_like(prev_rows), prev_rows)
    conv = _causal_conv(cu_ref[...].astype(F32), prev_rows, convw_ref[...])
    y_ref[:, Q_W:] = (gb_ref[...].astype(F32) * conv).astype(BF16)
    o_ref[...] = x_ref[...] + _dot(y_ref[...], wout_ref[...])


def _mixer0(x2d, bsz, seq, q, k, v, gb, cu, sinks, conv_w, w_out):
    n_rows = x2d.shape[0]
    tiles = seq // MIX_ROWS
    blk_per_tile = MIX_ROWS // ATTN_BLOCK
    sub_per_tile = MIX_ROWS // SUBLANES
    row = lambda b, i: (b * tiles + i, 0)
    prev_blk = lambda b, i: (jnp.maximum((b * tiles + i) * blk_per_tile - 1, 0), 0)
    prev_sub = lambda b, i: (jnp.maximum((b * tiles + i) * sub_per_tile - 1, 0), 0)
    return pl.pallas_call(
        _mixer0_kernel,
        grid=(bsz, tiles),
        in_specs=[
            pl.BlockSpec(memory_space=pltpu.SMEM),
            pl.BlockSpec((MIX_ROWS, Q_W), row),
            pl.BlockSpec((MIX_ROWS, 2 * KV_W), row),
            pl.BlockSpec((MIX_ROWS, 2 * KV_W), row),
            pl.BlockSpec((ATTN_BLOCK, 2 * KV_W), prev_blk),
            pl.BlockSpec((ATTN_BLOCK, 2 * KV_W), prev_blk),
            pl.BlockSpec((MIX_ROWS, CONV_CH), row),
            pl.BlockSpec((MIX_ROWS, CONV_CH), row),
            pl.BlockSpec((SUBLANES, CONV_CH), prev_sub),
            pl.BlockSpec((MIX_ROWS, D_MODEL), row),
            _const_spec((CONV0_WIDTH, CONV_CH)),
            _const_spec((D_MODEL, D_MODEL)),
        ],
        out_specs=pl.BlockSpec((MIX_ROWS, D_MODEL), row),
        out_shape=jax.ShapeDtypeStruct((n_rows, D_MODEL), F32),
        scratch_shapes=[
            pltpu.VMEM((MIX_ROWS, D_MODEL), BF16),
        ],
        compiler_params=_params(("parallel", "arbitrary")),
        name="mixer0",
    )(sinks.astype(F32), q, k, v, k, v, gb, cu, cu, x2d, conv_w.astype(F32), w_out.astype(BF16))


GATE_LANES = LANES


def _inproj1_kernel(x_ref, gain_ref, w_ref, wab_ref, convw_ref, qkv_ref, z_ref, ab_ref, halo_ref):
    @pl.when(pl.program_id(1) == 0)
    def _():
        halo_ref[...] = jnp.zeros_like(halo_ref)

    xn = _rms_norm_rows(x_ref[...], gain_ref[...]).astype(BF16)
    ones = _block_ones(MXU_DIM, DN_HEAD_DIM)

    def project(c):
        return _dot(xn, w_ref[:, c * DN_W:(c + 1) * DN_W])

    def finish(c, proj):
        cols = slice(c * DN_W, (c + 1) * DN_W)
        act = _silu(_causal_conv(proj, halo_ref[:, cols], convw_ref[:, cols]))
        halo_ref[:, cols] = proj[ROW_TILE - SUBLANES:, :]
        if c < 2:
            scale = float(DN_HEAD_DIM) if c == 0 else 1.0
            summer = ones * scale if c == 0 else ones
            sq = (act * act).astype(BF16)
            ss = jnp.concatenate([_dot(sq[:, c0:c0 + MXU_DIM], summer) for c0 in range(0, DN_W, MXU_DIM)],
                                 axis=1)
            act = act * lax.rsqrt(ss + EPS * scale)
        qkv_ref[:, cols] = act.astype(BF16)

    proj_q = project(0)
    proj_k = project(1)
    finish(0, proj_q)
    proj_v = project(2)
    finish(1, proj_k)
    z_ref[...] = project(3).astype(BF16)
    finish(2, proj_v)
    ab_ref[...] = _dot(xn, wab_ref[...])


def _inproj1(x2d, bsz, seq, gain, w_in, conv_w):
    n_rows = x2d.shape[0]
    w = w_in[:, :4 * DN_W].astype(BF16)
    wab = jnp.pad(w_in[:, 4 * DN_W:], ((0, 0), (0, GATE_LANES - 2 * DN_HEADS))).astype(BF16)
    tiles = seq // ROW_TILE
    row = lambda b, i: (b * tiles + i, 0)
    return pl.pallas_call(
        _inproj1_kernel,
        grid=(bsz, tiles),
        in_specs=[
            pl.BlockSpec((ROW_TILE, D_MODEL), row),
            _const_spec((1, D_MODEL)),
            _const_spec((D_MODEL, 4 * DN_W)),
            _const_spec((D_MODEL, GATE_LANES)),
            _const_spec((DN_CONV_WIDTH, 3 * DN_W)),
        ],
        out_specs=[
            pl.BlockSpec((ROW_TILE, 3 * DN_W), row),
            pl.BlockSpec((ROW_TILE, DN_W), row),
            pl.BlockSpec((ROW_TILE, GATE_LANES), row),
        ],
        out_shape=[
            jax.ShapeDtypeStruct((n_rows, 3 * DN_W), BF16),
            jax.ShapeDtypeStruct((n_rows, DN_W), BF16),
            jax.ShapeDtypeStruct((n_rows, GATE_LANES), F32),
        ],
        scratch_shapes=[
            pltpu.VMEM((SUBLANES, 3 * DN_W), F32),
        ],
        compiler_params=_params(("parallel", "arbitrary")),
        name="inproj1",
    )(x2d, gain.reshape(1, D_MODEL), w, wab, conv_w.astype(F32))


GDN_ROWS = MIX_ROWS
GDN_CHUNKS = GDN_ROWS // DN_CHUNK
GDN_GROUP = 2
SPLIT_TERMS = 3


def _split_bf16(v, terms):
    parts, rem = [], v
    for _ in range(terms):
        p = rem.astype(BF16)
        parts.append(p)
        rem = rem - p.astype(F32)
    return parts


def _softplus(v):
    return jnp.maximum(v, 0.0) + jnp.log(1.0 + jnp.exp(-jnp.abs(v)))


def _gdn_kernel(qkv_ref, z_ref, ab_ref, abt_ref, x_ref, gcol_ref, grow_ref, ogain_ref,
                wout_ref, o_ref, state_ref, gate_ref, o_acc_ref):
    @pl.when(pl.program_id(1) == 0)
    def _():
        state_ref[...] = jnp.zeros_like(state_ref)

    gcoef = gcol_ref[...]
    ab = ab_ref[...]
    gate_ref[:, 0:LANES] = gcoef[0:1, :] * _softplus(ab + gcoef[1:2, :])
    gate_ref[:, LANES:2 * LANES] = _sigmoid(ab)
    rcoef = grow_ref[...]

    ci = lax.broadcasted_iota(jnp.int32, (DN_CHUNK, DN_CHUNK), 0)
    cj = lax.broadcasted_iota(jnp.int32, (DN_CHUNK, DN_CHUNK), 1)
    lower = ci >= cj
    strict = ci > cj
    tril = lower.astype(BF16)
    triu = (ci <= cj).astype(BF16)

    eye = jnp.where(ci == cj, 1.0, 0.0)
    heads = range(DN_HEADS)

    def group_body(it, carry):
        chunks = [it * GDN_GROUP + j for j in range(GDN_GROUP)]
        rows = [pl.ds(pl.multiple_of(c * DN_CHUNK, DN_CHUNK), DN_CHUNK) for c in chunks]
        items = [(j, h) for j in range(GDN_GROUP) for h in heads]
        gc_col = [sum(_dot(tril, p) for p in _split_bf16(gate_ref[r, 0:LANES], SPLIT_TERMS))
                  for r in rows]
        g_row = [rcoef[:, 0:1] * _softplus(abt_ref[c] + rcoef[:, 1:2]) for c in chunks]
        gc_row = [sum(_dot(p, triu) for p in _split_bf16(g, SPLIT_TERMS)) for g in g_row]
        beta_c = [gate_ref[r, LANES:2 * LANES] for r in rows]

        def head_cols(j, h, group):
            return qkv_ref[rows[j], group * DN_W + h * LANES:group * DN_W + (h + 1) * LANES]

        q16 = [head_cols(j, h, 0) for j, h in items]
        k16 = [head_cols(j, h, 1) for j, h in items]
        k = [t.astype(F32) for t in k16]
        v = [head_cols(j, h, 2).astype(F32) for j, h in items]
        beta = [beta_c[j][:, h:h + 1] for j, h in items]
        gcc = [gc_col[j][:, DN_HEADS + h:DN_HEADS + h + 1] for j, h in items]
        gcr = [gc_row[j][DN_HEADS + h:DN_HEADS + h + 1, :] for j, h in items]
        gc_last = [g[DN_CHUNK - 1:DN_CHUNK, :] for g in gcc]
        n = range(len(items))
        eg = [jnp.exp(g) for g in gcc]
        gamma = [jnp.exp(jnp.where(lower, gcc[i] - gcr[i], -jnp.inf)) for i in n]
        kb = [k[i] * beta[i] for i in n]
        s1 = [_dot_nt(jnp.concatenate([kb[i].astype(BF16), q16[i]], axis=0), k16[i]) for i in n]
        attn = [(s1[i][DN_CHUNK:] * gamma[i]).astype(BF16) for i in n]
        b = [jnp.where(strict, -(s1[i][:DN_CHUNK] * gamma[i]), 0.0).astype(BF16) for i in n]
        p = [eye + b[i].astype(F32) for i in n]
        b = [_dot(b[i], b[i]).astype(BF16) for i in n]
        power = 2
        while power < DN_CHUNK:
            last = 2 * power >= DN_CHUNK
            lhs = [p[i].astype(BF16) if last else jnp.concatenate([b[i], p[i].astype(BF16)], axis=0)
                   for i in n]
            prod = [_dot(lhs[i], b[i]) for i in n]
            if last:
                p = [p[i] + prod[i] for i in n]
            else:
                b = [prod[i][:DN_CHUNK].astype(BF16) for i in n]
                p = [p[i] + prod[i][DN_CHUNK:] for i in n]
            power *= 2
        rhs = [jnp.concatenate([v[i] * beta[i], kb[i] * eg[i]], axis=1).astype(BF16) for i in n]
        uw = [_dot(p[i].astype(BF16), rhs[i]).astype(BF16) for i in n]
        kd = [(k[i] * jnp.exp(gc_last[i] - gcc[i])).astype(BF16) for i in n]
        nm = [_dot_tn(kd[i], uw[i]) for i in n]
        oa = [_dot(attn[i], uw[i]) for i in n]
        lhs = [jnp.concatenate([nm[i][:, LANES:], q16[i].astype(F32) * eg[i] - oa[i][:, LANES:]],
                               axis=0).astype(BF16) for i in n]
        decay = [jnp.exp(g) for g in gc_last]
        state = [state_ref[h] for h in heads]
        for j in range(GDN_GROUP):
            idx = [j * DN_HEADS + h for h in heads]
            res = [_dot(lhs[i], state[h].astype(BF16)) for h, i in zip(heads, idx)]
            state = [decay[i] * state[h] - res[h][:DN_HEAD_DIM] + nm[i][:, :LANES] for h, i in zip(heads, idx)]
            for h, i in zip(heads, idx):
                o_acc_ref[rows[j], h * LANES:(h + 1) * LANES] = res[h][DN_HEAD_DIM:] + oa[i][:, :LANES]
        for h in heads:
            state_ref[h] = state[h]
        return carry

    lax.fori_loop(0, GDN_CHUNKS // GDN_GROUP, group_body, 0)

    o = o_acc_ref[...]
    ones = _block_ones(MXU_DIM, DN_HEAD_DIM)
    sq = (o * o).astype(BF16)
    ss = jnp.concatenate([_dot(sq[:, c0:c0 + MXU_DIM], ones) for c0 in range(0, DN_W, MXU_DIM)], axis=1)
    y = o * lax.rsqrt(ss * (1.0 / DN_HEAD_DIM) + EPS) * ogain_ref[...] * _silu(z_ref[...].astype(F32))
    o_ref[...] = x_ref[...] + _dot(y.astype(BF16), wout_ref[...])


def _gdn(x2d, bsz, seq, qkv, z, ab, a_log, dt_bias, o_gain, w_out):
    n_rows = x2d.shape[0]
    n_chunks = n_rows // DN_CHUNK
    abt = ab[:, :2 * DN_HEADS].reshape(n_chunks, DN_CHUNK, 2 * DN_HEADS).transpose(0, 2, 1)
    neg_a = -jnp.exp(a_log.astype(F32))
    pad_lo = jnp.zeros((DN_HEADS,), F32)
    gcol = jnp.zeros((2, LANES), F32)
    gcol = gcol.at[0, DN_HEADS:2 * DN_HEADS].set(neg_a).at[1, DN_HEADS:2 * DN_HEADS].set(dt_bias.astype(F32))
    grow = jnp.stack([jnp.concatenate([pad_lo, neg_a]), jnp.concatenate([pad_lo, dt_bias.astype(F32)])], axis=1)
    ogain = jnp.tile(o_gain.astype(F32), DN_HEADS).reshape(1, DN_W)
    tiles = seq // GDN_ROWS
    row = lambda b, i: (b * tiles + i, 0)
    return pl.pallas_call(
        _gdn_kernel,
        grid=(bsz, tiles),
        in_specs=[
            pl.BlockSpec((GDN_ROWS, 3 * DN_W), row),
            pl.BlockSpec((GDN_ROWS, DN_W), row),
            pl.BlockSpec((GDN_ROWS, GATE_LANES), row),
            pl.BlockSpec((GDN_CHUNKS, 2 * DN_HEADS, DN_CHUNK), lambda b, i: (b * tiles + i, 0, 0)),
            pl.BlockSpec((GDN_ROWS, D_MODEL), row),
            _const_spec((2, LANES)),
            _const_spec((2 * DN_HEADS, 2)),
            _const_spec((1, DN_W)),
            _const_spec((DN_W, D_MODEL)),
        ],
        out_specs=pl.BlockSpec((GDN_ROWS, D_MODEL), row),
        out_shape=jax.ShapeDtypeStruct((n_rows, D_MODEL), F32),
        scratch_shapes=[
            pltpu.VMEM((DN_HEADS, DN_HEAD_DIM, DN_HEAD_DIM), F32),
            pltpu.VMEM((GDN_ROWS, 2 * LANES), F32),
            pltpu.VMEM((GDN_ROWS, DN_W), F32),
        ],
        compiler_params=_params(("parallel", "arbitrary")),
        name="gdn",
    )(qkv, z, ab, abt, x2d, gcol, grow, ogain, w_out.astype(BF16))


def kernel(x, even_norm, even_w_in, even_q_gain, even_k_gain, even_sinks, even_conv_w, even_w_out, odd_norm, odd_w_in, odd_conv_w, odd_a_log, odd_dt_bias, odd_o_gain, odd_w_out, ffn_norm, ffn_w_gate_up, ffn_w_down):
    bsz, seq, _ = x.shape
    x2d = x.reshape(bsz * seq, D_MODEL)
    q, k, v, gb, cu = _inproj0(x2d, seq, even_norm[0], even_w_in[0], even_q_gain[0], even_k_gain[0])
    x2d = _mixer0(x2d, bsz, seq, q, k, v, gb, cu, even_sinks[0], even_conv_w[0], even_w_out[0])
    x2d = _ffn(x2d, ffn_norm[0], ffn_w_gate_up[0], ffn_w_down[0])
    qkv, z, ab = _inproj1(x2d, bsz, seq, odd_norm[0], odd_w_in[0], odd_conv_w[0])
    x2d = _gdn(x2d, bsz, seq, qkv, z, ab, odd_a_log[0], odd_dt_bias[0], odd_o_gain[0], odd_w_out[0])
    x2d = _ffn(x2d, ffn_norm[1], ffn_w_gate_up[1], ffn_w_down[1])
    return x2d.reshape(bsz, seq, D_MODEL)
```

```python
import functools

import jax
import jax.numpy as jnp
from jax import lax
from jax.experimental import pallas as pl
from jax.experimental.pallas import tpu as pltpu

D_MODEL = 1024
D_FF = 2816
EPS = 1e-6

HEAD_DIM = 64
ATTN_HEADS = 8
ATTN_KV_HEADS = 2
ATTN_BLOCK = 128
ROPE_THETA = 10000.0
CONV_CH = 512
Q_W = ATTN_HEADS * HEAD_DIM
KV_W = ATTN_KV_HEADS * HEAD_DIM
DN_HEAD_DIM = 128
DN_HEADS = 8
DN_W = DN_HEADS * DN_HEAD_DIM
DN_CONV_WIDTH = 4
DN_CHUNK = 64

LANES = 128
SUBLANES = 8
MXU_DIM = 256
VMEM_LIMIT_BYTES = 56 * 1024 * 1024

ROW_TILE = 512
FFN_ROWS = 512
FFN_SPLITS = (0, 1536, D_FF)
MIX_ROWS = 512
NEG_BIG = -1e30

BF16 = jnp.bfloat16
F32 = jnp.float32


def _rms_norm_rows(xf, gain):
    ms = jnp.mean(xf * xf, axis=-1, keepdims=True)
    return xf * lax.rsqrt(ms + EPS) * gain


def _sigmoid(v):
    return 1.0 / (1.0 + jnp.exp(-v))


def _silu(v):
    h = 0.5 * v
    return h + h * jnp.tanh(h)


def _dot(a, b):
    return jnp.dot(a, b, preferred_element_type=F32)


def _dot_nt(a, b):
    return lax.dot_general(a, b, (((1,), (1,)), ((), ())), preferred_element_type=F32)


def _dot_tn(a, b):
    return lax.dot_general(a, b, (((0,), (0,)), ((), ())), preferred_element_type=F32)


def _const_spec(shape):
    nd = len(shape)
    return pl.BlockSpec(shape, lambda *_: (0,) * nd, pipeline_mode=pl.Buffered(1))


def _block_ones(n, block):
    r = lax.broadcasted_iota(jnp.int32, (n, n), 0) // block
    c = lax.broadcasted_iota(jnp.int32, (n, n), 1) // block
    return (r == c).astype(BF16)


def _group_sumsq(v, block, scale):
    ones = _block_ones(MXU_DIM, block) * scale
    sq = (v * v).astype(BF16)
    parts = [_dot(sq[:, c:c + MXU_DIM], ones) for c in range(0, v.shape[1], MXU_DIM)]
    return parts[0] if len(parts) == 1 else jnp.concatenate(parts, axis=1)


def _causal_conv(cur, prev, w):
    width = w.shape[0]
    sub = lax.broadcasted_iota(jnp.int32, prev.shape, 0)
    out = w[width - 1:width, :] * cur
    for s in range(1, width):
        rolled = pltpu.roll(cur, s, axis=0)
        head = jnp.where(sub < s, pltpu.roll(prev, s, axis=0), rolled[0:SUBLANES])
        shifted = jnp.concatenate([head, rolled[SUBLANES:]], axis=0)
        out = out + w[width - 1 - s:width - s, :] * shifted
    return out


def _params(sem):
    return pltpu.CompilerParams(dimension_semantics=sem, vmem_limit_bytes=VMEM_LIMIT_BYTES)


def _ffn_rows(xf, gain_ref, wgu_ref, wd_ref):
    xn = _rms_norm_rows(xf, gain_ref[...]).astype(BF16)
    acc = xf
    for lo, hi in zip(FFN_SPLITS[:-1], FFN_SPLITS[1:]):
        g = _dot(xn, wgu_ref[:, lo:hi])
        u = _dot(xn, wgu_ref[:, D_FF + lo:D_FF + hi])
        act = (_silu(g) * u).astype(BF16)
        acc = acc + _dot(act, wd_ref[lo:hi, :])
    return acc


def _ffn_kernel(x_ref, gain_ref, wgu_ref, wd_ref, o_ref):
    o_ref[...] = _ffn_rows(x_ref[...], gain_ref, wgu_ref, wd_ref)


def _ffn(x2d, gain, w_gate_up, w_down):
    n_rows = x2d.shape[0]
    wgu = w_gate_up.astype(BF16)
    wd = w_down.astype(BF16)
    return pl.pallas_call(
        _ffn_kernel,
        grid=(n_rows // FFN_ROWS,),
        in_specs=[
            pl.BlockSpec((FFN_ROWS, D_MODEL), lambda i: (i, 0)),
            _const_spec((1, D_MODEL)),
            _const_spec((D_MODEL, 2 * D_FF)),
            _const_spec((D_FF, D_MODEL)),
        ],
        out_specs=pl.BlockSpec((FFN_ROWS, D_MODEL), lambda i: (i, 0)),
        out_shape=jax.ShapeDtypeStruct((n_rows, D_MODEL), F32),
        compiler_params=_params(("parallel",)),
        name="ffn",
    )(x2d, gain.reshape(1, D_MODEL), wgu, wd)


QK_COLS = Q_W + 2 * KV_W


def _inproj0_kernel(x_ref, gain_ref, w_ref, qkgain_ref, cos_ref, sin_ref,
                    q_ref, k_ref, v_ref, gb_ref, cu_ref):
    xn = _rms_norm_rows(x_ref[...], gain_ref[...]).astype(BF16)
    qk = _dot(xn, w_ref[:, :QK_COLS])
    c0 = QK_COLS
    v_ref[...] = _dot(xn, w_ref[:, c0:c0 + 2 * KV_W]).astype(BF16)
    c0 += 2 * KV_W
    ms = _group_sumsq(qk, HEAD_DIM, 1.0 / HEAD_DIM)
    gb_ref[...] = _dot(xn, w_ref[:, c0:c0 + CONV_CH]).astype(BF16)
    c0 += CONV_CH
    qk = qk * lax.rsqrt(ms + EPS) * qkgain_ref[...]
    gc = _dot(xn, w_ref[:, c0:c0 + CONV_CH])
    xin = _dot(xn, w_ref[:, c0 + CONV_CH:c0 + 2 * CONV_CH])
    cu_ref[...] = (gc * xin).astype(BF16)
    reps = QK_COLS // LANES
    cos = jnp.concatenate([cos_ref[...]] * reps, axis=1)
    sin = jnp.concatenate([sin_ref[...]] * reps, axis=1)
    half = HEAD_DIM // 2
    lane = lax.broadcasted_iota(jnp.int32, qk.shape, 1)
    partner = jnp.where((lane % HEAD_DIM) < half,
                        pltpu.roll(qk, QK_COLS - half, axis=1),
                        pltpu.roll(qk, half, axis=1))
    qk = qk * cos + partner * sin
    q_ref[...] = qk[:, :Q_W].astype(BF16)
    k_ref[...] = qk[:, Q_W:].astype(BF16)


def _dup_heads(w, n_heads, dim):
    w3 = w.reshape(w.shape[0], n_heads, 1, dim)
    return jnp.concatenate([w3, w3], axis=2).reshape(w.shape[0], 2 * n_heads * dim)


def _inproj0(x2d, seq, gain, w_in, q_gain, k_gain):
    n_rows = x2d.shape[0]
    wq = w_in[:, :Q_W]
    wk = _dup_heads(w_in[:, Q_W:Q_W + KV_W], ATTN_KV_HEADS, HEAD_DIM)
    wv = _dup_heads(w_in[:, Q_W + KV_W:Q_W + 2 * KV_W], ATTN_KV_HEADS, HEAD_DIM)
    w = jnp.concatenate([wq, wk, wv, w_in[:, Q_W + 2 * KV_W:]], axis=1).astype(BF16)
    n_cols = w.shape[1]
    qkgain = jnp.concatenate([jnp.tile(q_gain * (HEAD_DIM ** -0.5), ATTN_HEADS),
                              jnp.tile(k_gain, 2 * ATTN_KV_HEADS)]).reshape(1, QK_COLS).astype(F32)
    inv_freq = ROPE_THETA ** (-jnp.arange(0, HEAD_DIM, 2, dtype=F32) / HEAD_DIM)
    ang = jnp.arange(seq, dtype=F32)[:, None] * inv_freq[None, :]
    cos, sin = jnp.cos(ang), jnp.sin(ang)
    cos_t = jnp.tile(jnp.concatenate([cos, cos], axis=1), (1, LANES // HEAD_DIM))
    sin_t = jnp.tile(jnp.concatenate([-sin, sin], axis=1), (1, LANES // HEAD_DIM))
    tiles_per_seq = seq // ROW_TILE
    row = lambda i: (i, 0)
    pos = lambda i: (i % tiles_per_seq, 0)
    return pl.pallas_call(
        _inproj0_kernel,
        grid=(n_rows // ROW_TILE,),
        in_specs=[
            pl.BlockSpec((ROW_TILE, D_MODEL), row),
            _const_spec((1, D_MODEL)),
            _const_spec((D_MODEL, n_cols)),
            _const_spec((1, QK_COLS)),
            pl.BlockSpec((ROW_TILE, LANES), pos),
            pl.BlockSpec((ROW_TILE, LANES), pos),
        ],
        out_specs=[
            pl.BlockSpec((ROW_TILE, Q_W), row),
            pl.BlockSpec((ROW_TILE, 2 * KV_W), row),
            pl.BlockSpec((ROW_TILE, 2 * KV_W), row),
            pl.BlockSpec((ROW_TILE, CONV_CH), row),
            pl.BlockSpec((ROW_TILE, CONV_CH), row),
        ],
        out_shape=[
            jax.ShapeDtypeStruct((n_rows, Q_W), BF16),
            jax.ShapeDtypeStruct((n_rows, 2 * KV_W), BF16),
            jax.ShapeDtypeStruct((n_rows, 2 * KV_W), BF16),
            jax.ShapeDtypeStruct((n_rows, CONV_CH), BF16),
            jax.ShapeDtypeStruct((n_rows, CONV_CH), BF16),
        ],
        compiler_params=_params(("parallel",)),
        name="inproj0",
    )(x2d, gain.reshape(1, D_MODEL), w, qkgain, cos_t, sin_t)


CONV0_WIDTH = 3


def _mixer0_kernel(sinks_ref, q_ref, k_ref, v_ref, kp_ref, vp_ref, gb_ref, cu_ref, cup_ref,
                   x_ref, convw_ref, wout_ref, o_ref, y_ref):
    first = pl.program_id(1) == 0
    n_blocks = MIX_ROWS // ATTN_BLOCK
    grp = ATTN_HEADS // ATTN_KV_HEADS
    r = lax.broadcasted_iota(jnp.int32, (ATTN_BLOCK, 2 * ATTN_BLOCK), 0)
    c = lax.broadcasted_iota(jnp.int32, (ATTN_BLOCK, 2 * ATTN_BLOCK), 1)
    band = (c - r >= 1) & (c - r <= ATTN_BLOCK)
    lane = lax.broadcasted_iota(jnp.int32, (ATTN_BLOCK, LANES), 1)
    low_half = lane < HEAD_DIM
    block_rows = [slice(j * ATTN_BLOCK, (j + 1) * ATTN_BLOCK) for j in range(n_blocks)]

    def keys_values(ref, prev_ref, j, kv):
        before = prev_ref[:, kv * LANES:(kv + 1) * LANES] if j == 0 else ref[block_rows[j - 1], kv * LANES:(kv + 1) * LANES]
        return jnp.concatenate([before, ref[block_rows[j], kv * LANES:(kv + 1) * LANES]], axis=0)

    def head_queries(j, head):
        q2 = q_ref[block_rows[j], (head // 2) * LANES:(head // 2 + 1) * LANES]
        keep = low_half if head % 2 == 0 else jnp.logical_not(low_half)
        return jnp.where(keep, q2, jnp.zeros_like(q2))

    first_valid = band & ((c >= ATTN_BLOCK) | jnp.logical_not(first))
    heads = range(ATTN_HEADS)
    for j in range(n_blocks):
        valid = first_valid if j == 0 else band
        kk = [keys_values(k_ref, kp_ref, j, kv) for kv in range(ATTN_KV_HEADS)]
        vv = [keys_values(v_ref, vp_ref, j, kv) for kv in range(ATTN_KV_HEADS)]
        s = [jnp.where(valid, _dot_nt(head_queries(j, h), kk[h // grp]), NEG_BIG) for h in heads]
        m = [jnp.maximum(jnp.max(s[h], axis=-1, keepdims=True), sinks_ref[h]) for h in heads]
        p = [jnp.exp(s[h] - m[h]) for h in heads]
        denom = [jnp.sum(p[h], axis=-1, keepdims=True) + jnp.exp(sinks_ref[h] - m[h]) for h in heads]
        o = [_dot(p[h].astype(BF16), vv[h // grp]) * (1.0 / denom[h]) for h in heads]
        for pair in range(ATTN_HEADS // 2):
            y_ref[block_rows[j], pair * LANES:(pair + 1) * LANES] = jnp.where(
                low_half, o[2 * pair], o[2 * pair + 1]).astype(BF16)
    prev_rows = cup_ref[...].astype(F32)
    prev_rows = jnp.where(first, jnp.zeros_like(prev_rows), prev_rows)
    conv = _causal_conv(cu_ref[...].astype(F32), prev_rows, convw_ref[...])
    y_ref[:, Q_W:] = (gb_ref[...].astype(F32) * conv).astype(BF16)
    o_ref[...] = x_ref[...] + _dot(y_ref[...], wout_ref[...])


def _mixer0(x2d, bsz, seq, q, k, v, gb, cu, sinks, conv_w, w_out):
    n_rows = x2d.shape[0]
    tiles = seq // MIX_ROWS
    blk_per_tile = MIX_ROWS // ATTN_BLOCK
    sub_per_tile = MIX_ROWS // SUBLANES
    row = lambda b, i: (b * tiles + i, 0)
    prev_blk = lambda b, i: (jnp.maximum((b * tiles + i) * blk_per_tile - 1, 0), 0)
    prev_sub = lambda b, i: (jnp.maximum((b * tiles + i) * sub_per_tile - 1, 0), 0)
    return pl.pallas_call(
        _mixer0_kernel,
        grid=(bsz, tiles),
        in_specs=[
            pl.BlockSpec(memory_space=pltpu.SMEM),
            pl.BlockSpec((MIX_ROWS, Q_W), row),
            pl.BlockSpec((MIX_ROWS, 2 * KV_W), row),
            pl.BlockSpec((MIX_ROWS, 2 * KV_W), row),
            pl.BlockSpec((ATTN_BLOCK, 2 * KV_W), prev_blk),
            pl.BlockSpec((ATTN_BLOCK, 2 * KV_W), prev_blk),
            pl.BlockSpec((MIX_ROWS, CONV_CH), row),
            pl.BlockSpec((MIX_ROWS, CONV_CH), row),
            pl.BlockSpec((SUBLANES, CONV_CH), prev_sub),
            pl.BlockSpec((MIX_ROWS, D_MODEL), row),
            _const_spec((CONV0_WIDTH, CONV_CH)),
            _const_spec((D_MODEL, D_MODEL)),
        ],
        out_specs=pl.BlockSpec((MIX_ROWS, D_MODEL), row),
        out_shape=jax.ShapeDtypeStruct((n_rows, D_MODEL), F32),
        scratch_shapes=[
            pltpu.VMEM((MIX_ROWS, D_MODEL), BF16),
        ],
        compiler_params=_params(("parallel", "arbitrary")),
        name="mixer0",
    )(sinks.astype(F32), q, k, v, k, v, gb, cu, cu, x2d, conv_w.astype(F32), w_out.astype(BF16))


GATE_LANES = LANES


def _inproj1_kernel(x_ref, gain_ref, w_ref, wab_ref, convw_ref, qkv_ref, z_ref, ab_ref, halo_ref):
    @pl.when(pl.program_id(1) == 0)
    def _():
        halo_ref[...] = jnp.zeros_like(halo_ref)

    xn = _rms_norm_rows(x_ref[...], gain_ref[...]).astype(BF16)
    ones = _block_ones(MXU_DIM, DN_HEAD_DIM)

    def project(c):
        return _dot(xn, w_ref[:, c * DN_W:(c + 1) * DN_W])

    def finish(c, proj):
        cols = slice(c * DN_W, (c + 1) * DN_W)
        act = _silu(_causal_conv(proj, halo_ref[:, cols], convw_ref[:, cols]))
        halo_ref[:, cols] = proj[ROW_TILE - SUBLANES:, :]
        if c < 2:
            scale = float(DN_HEAD_DIM) if c == 0 else 1.0
            summer = ones * scale if c == 0 else ones
            sq = (act * act).astype(BF16)
            ss = jnp.concatenate([_dot(sq[:, c0:c0 + MXU_DIM], summer) for c0 in range(0, DN_W, MXU_DIM)],
                                 axis=1)
            act = act * lax.rsqrt(ss + EPS * scale)
        qkv_ref[:, cols] = act.astype(BF16)

    proj_q = project(0)
    proj_k = project(1)
    finish(0, proj_q)
    proj_v = project(2)
    finish(1, proj_k)
    z_ref[...] = project(3).astype(BF16)
    finish(2, proj_v)
    ab_ref[...] = _dot(xn, wab_ref[...])


def _inproj1(x2d, bsz, seq, gain, w_in, conv_w):
    n_rows = x2d.shape[0]
    w = w_in[:, :4 * DN_W].astype(BF16)
    wab = jnp.pad(w_in[:, 4 * DN_W:], ((0, 0), (0, GATE_LANES - 2 * DN_HEADS))).astype(BF16)
    tiles = seq // ROW_TILE
    row = lambda b, i: (b * tiles + i, 0)
    return pl.pallas_call(
        _inproj1_kernel,
        grid=(bsz, tiles),
        in_specs=[
            pl.BlockSpec((ROW_TILE, D_MODEL), row),
            _const_spec((1, D_MODEL)),
            _const_spec((D_MODEL, 4 * DN_W)),
            _const_spec((D_MODEL, GATE_LANES)),
            _const_spec((DN_CONV_WIDTH, 3 * DN_W)),
        ],
        out_specs=[
            pl.BlockSpec((ROW_TILE, 3 * DN_W), row),
            pl.BlockSpec((ROW_TILE, DN_W), row),
            pl.BlockSpec((ROW_TILE, GATE_LANES), row),
        ],
        out_shape=[
            jax.ShapeDtypeStruct((n_rows, 3 * DN_W), BF16),
            jax.ShapeDtypeStruct((n_rows, DN_W), BF16),
            jax.ShapeDtypeStruct((n_rows, GATE_LANES), F32),
        ],
        scratch_shapes=[
            pltpu.VMEM((SUBLANES, 3 * DN_W), F32),
        ],
        compiler_params=_params(("parallel", "arbitrary")),
        name="inproj1",
    )(x2d, gain.reshape(1, D_MODEL), w, wab, conv_w.astype(F32))


GDN_ROWS = MIX_ROWS
GDN_CHUNKS = GDN_ROWS // DN_CHUNK
GDN_GROUP = 4
SPLIT_TERMS = 3


def _split_bf16(v, terms):
    parts, rem = [], v
    for _ in range(terms):
        p = rem.astype(BF16)
        parts.append(p)
        rem = rem - p.astype(F32)
    return parts


def _softplus(v):
    return jnp.maximum(v, 0.0) + jnp.log(1.0 + jnp.exp(-jnp.abs(v)))


def _gdn_kernel(qkv_ref, z_ref, ab_ref, abt_ref, x_ref, gcol_ref, grow_ref, ogain_ref,
                wout_ref, o_ref, state_ref, gate_ref, o_acc_ref):
    @pl.when(pl.program_id(1) == 0)
    def _():
        state_ref[...] = jnp.zeros_like(state_ref)

    gcoef = gcol_ref[...]
    ab = ab_ref[...]
    gate_ref[:, 0:LANES] = gcoef[0:1, :] * _softplus(ab + gcoef[1:2, :])
    gate_ref[:, LANES:2 * LANES] = _sigmoid(ab)
    rcoef = grow_ref[...]

    ci = lax.broadcasted_iota(jnp.int32, (DN_CHUNK, LANES), 0)
    lane = lax.broadcasted_iota(jnp.int32, (DN_CHUNK, LANES), 1)
    cj = lane % DN_CHUNK
    left = lane < DN_CHUNK
    left_row = left[0:1, :]
    lower = ci >= cj
    strict = ci > cj
    eye = jnp.where(ci == cj, 1.0, 0.0)
    tril = lower[:, :DN_CHUNK].astype(BF16)
    triu2 = (ci <= cj).astype(BF16)
    heads = range(DN_HEADS)
    pairs = range(DN_HEADS // 2)
    zero_k = jnp.zeros((DN_CHUNK, LANES), BF16)
    zero_uw = jnp.zeros((DN_CHUNK, 2 * LANES), BF16)
    zero_s = jnp.zeros((DN_HEAD_DIM, DN_HEAD_DIM), BF16)

    def pair_diag(mat):
        return jnp.concatenate([jnp.where(left, mat, jnp.zeros_like(mat)),
                                jnp.where(left, jnp.zeros_like(mat), mat)], axis=0)

    def group_body(it, carry):
        chunks = [it * GDN_GROUP + j for j in range(GDN_GROUP)]
        rows = [pl.ds(pl.multiple_of(c * DN_CHUNK, DN_CHUNK), DN_CHUNK) for c in chunks]
        items = [(j, h) for j in range(GDN_GROUP) for h in heads]
        gc_col = [sum(_dot(tril, p) for p in _split_bf16(gate_ref[r, 0:LANES], SPLIT_TERMS))
                  for r in rows]
        g_row = [rcoef[:, 0:1] * _softplus(abt_ref[c] + rcoef[:, 1:2]) for c in chunks]
        gc_row = [sum(_dot(p, triu2) for p in _split_bf16(g, SPLIT_TERMS)) for g in g_row]
        beta_c = [gate_ref[r, LANES:2 * LANES] for r in rows]

        def head_cols(j, h, group):
            return qkv_ref[rows[j], group * DN_W + h * LANES:group * DN_W + (h + 1) * LANES]

        q16 = [head_cols(j, h, 0) for j, h in items]
        k16 = [head_cols(j, h, 1) for j, h in items]
        k = [t.astype(F32) for t in k16]
        v = [head_cols(j, h, 2).astype(F32) for j, h in items]
        beta = [beta_c[j][:, h:h + 1] for j, h in items]
        gcc = [gc_col[j][:, DN_HEADS + h:DN_HEADS + h + 1] for j, h in items]
        gc_last = [g[DN_CHUNK - 1:DN_CHUNK, :] for g in gcc]
        n = range(len(items))
        eg = [jnp.exp(g) for g in gcc]
        kb = [k[i] * beta[i] for i in n]
        kb16 = [t.astype(BF16) for t in kb]
        pair_items = [(j, p) for j in range(GDN_GROUP) for p in pairs]
        m = range(len(pair_items))
        ia = [j * DN_HEADS + 2 * p for j, p in pair_items]
        ib = [i + 1 for i in ia]
        gcc2 = [jnp.where(left, gcc[ia[t]], gcc[ib[t]]) for t in m]
        gcr2 = [jnp.where(left_row, gc_row[j][DN_HEADS + 2 * p:DN_HEADS + 2 * p + 1, :],
                          gc_row[j][DN_HEADS + 2 * p + 1:DN_HEADS + 2 * p + 2, :]) for j, p in pair_items]
        gamma = [jnp.exp(jnp.where(lower, gcc2[t] - gcr2[t], -jnp.inf)) for t in m]
        s1 = [_dot_nt(jnp.concatenate([jnp.concatenate([kb16[ia[t]], kb16[ib[t]]], axis=1),
                                       jnp.concatenate([q16[ia[t]], q16[ib[t]]], axis=1)], axis=0),
                      jnp.concatenate([jnp.concatenate([k16[ia[t]], zero_k], axis=1),
                                       jnp.concatenate([zero_k, k16[ib[t]]], axis=1)], axis=0))
              for t in m]
        attn = [(s1[t][DN_CHUNK:] * gamma[t]).astype(BF16) for t in m]
        b = [jnp.where(strict, -(s1[t][:DN_CHUNK] * gamma[t]), 0.0).astype(BF16) for t in m]
        p = [eye + b[t].astype(F32) for t in m]
        b = [_dot(b[t], pair_diag(b[t])).astype(BF16) for t in m]
        power = 2
        while power < DN_CHUNK:
            last = 2 * power >= DN_CHUNK
            lhs = [p[t].astype(BF16) if last else jnp.concatenate([b[t], p[t].astype(BF16)], axis=0)
                   for t in m]
            prod = [_dot(lhs[t], pair_diag(b[t])) for t in m]
            if last:
                p = [p[t] + prod[t] for t in m]
            else:
                b = [prod[t][:DN_CHUNK].astype(BF16) for t in m]
                p = [p[t] + prod[t][DN_CHUNK:] for t in m]
            power *= 2
        t16 = [p[t].astype(BF16) for t in m]
        pair_of = {i: (t, 0) for t, i in enumerate(ia)}
        pair_of.update({i: (t, 1) for t, i in enumerate(ib)})

        def own_rows(i, mat, zero):
            return jnp.concatenate([mat, zero] if pair_of[i][1] == 0 else [zero, mat], axis=0)

        rhs = [jnp.concatenate([v[i] * beta[i], kb[i] * eg[i]], axis=1).astype(BF16) for i in n]
        uw = [_dot(t16[pair_of[i][0]], own_rows(i, rhs[i], zero_uw)).astype(BF16) for i in n]
        kd = [(k[i] * jnp.exp(gc_last[i] - gcc[i])).astype(BF16) for i in n]
        nm = [_dot_tn(kd[i], uw[i]) for i in n]
        oa = [_dot(attn[pair_of[i][0]], own_rows(i, uw[i], zero_uw)) for i in n]
        lhs = [jnp.concatenate([nm[i][:, LANES:], q16[i].astype(F32) * eg[i] - oa[i][:, LANES:]],
                               axis=0).astype(BF16) for i in n]
        decay = [jnp.exp(g) for g in gc_last]
        state = [state_ref[h] for h in heads]
        for j in range(GDN_GROUP):
            new_state = []
            for p_ in pairs:
                a, b_ = 2 * p_, 2 * p_ + 1
                i_a, i_b = j * DN_HEADS + a, j * DN_HEADS + b_
                s_diag = jnp.concatenate(
                    [jnp.concatenate([state[a].astype(BF16), zero_s], axis=1),
                     jnp.concatenate([zero_s, state[b_].astype(BF16)], axis=1)], axis=0)
                res = _dot(jnp.concatenate([lhs[i_a], lhs[i_b]], axis=1), s_diag)
                for h, i, cols in ((a, i_a, slice(0, LANES)), (b_, i_b, slice(LANES, 2 * LANES))):
                    new_state.append(decay[i] * state[h] - res[:DN_HEAD_DIM, cols] + nm[i][:, :LANES])
                    o_acc_ref[rows[j], h * LANES:(h + 1) * LANES] = res[DN_HEAD_DIM:, cols] + oa[i][:, :LANES]
            state = new_state
        for h in heads:
            state_ref[h] = state[h]
        return carry

    lax.fori_loop(0, GDN_CHUNKS // GDN_GROUP, group_body, 0)

    o = o_acc_ref[...]
    ms = _group_sumsq(o, DN_HEAD_DIM, 1.0 / DN_HEAD_DIM)
    y = o * lax.rsqrt(ms + EPS) * ogain_ref[...] * _silu(z_ref[...].astype(F32))
    o_ref[...] = x_ref[...] + _dot(y.astype(BF16), wout_ref[...])


def _gdn(x2d, bsz, seq, qkv, z, ab, a_log, dt_bias, o_gain, w_out):
    n_rows = x2d.shape[0]
    n_chunks = n_rows // DN_CHUNK
    abt = ab[:, :2 * DN_HEADS].reshape(n_chunks, DN_CHUNK, 2 * DN_HEADS).transpose(0, 2, 1)
    neg_a = -jnp.exp(a_log.astype(F32))
    pad_lo = jnp.zeros((DN_HEADS,), F32)
    gcol = jnp.zeros((2, LANES), F32)
    gcol = gcol.at[0, DN_HEADS:2 * DN_HEADS].set(neg_a).at[1, DN_HEADS:2 * DN_HEADS].set(dt_bias.astype(F32))
    grow = jnp.stack([jnp.concatenate([pad_lo, neg_a]), jnp.concatenate([pad_lo, dt_bias.astype(F32)])], axis=1)
    ogain = jnp.tile(o_gain.astype(F32), DN_HEADS).reshape(1, DN_W)
    tiles = seq // GDN_ROWS
    row = lambda b, i: (b * tiles + i, 0)
    return pl.pallas_call(
        _gdn_kernel,
        grid=(bsz, tiles),
        in_specs=[
            pl.BlockSpec((GDN_ROWS, 3 * DN_W), row),
            pl.BlockSpec((GDN_ROWS, DN_W), row),
            pl.BlockSpec((GDN_ROWS, GATE_LANES), row),
            pl.BlockSpec((GDN_CHUNKS, 2 * DN_HEADS, DN_CHUNK), lambda b, i: (b * tiles + i, 0, 0)),
            pl.BlockSpec((GDN_ROWS, D_MODEL), row),
            _const_spec((2, LANES)),
            _const_spec((2 * DN_HEADS, 2)),
            _const_spec((1, DN_W)),
            _const_spec((DN_W, D_MODEL)),
        ],
        out_specs=pl.BlockSpec((GDN_ROWS, D_MODEL), row),
        out_shape=jax.ShapeDtypeStruct((n_rows, D_MODEL), F32),
        scratch_shapes=[
            pltpu.VMEM((DN_HEADS, DN_HEAD_DIM, DN_HEAD_DIM), F32),
            pltpu.VMEM((GDN_ROWS, 2 * LANES), F32),
            pltpu.VMEM((GDN_ROWS, DN_W), F32),
        ],
        compiler_params=_params(("parallel", "arbitrary")),
        name="gdn",
    )(qkv, z, ab, abt, x2d, gcol, grow, ogain, w_out.astype(BF16))


def kernel(x, even_norm, even_w_in, even_q_gain, even_k_gain, even_sinks, even_conv_w, even_w_out, odd_norm, odd_w_in, odd_conv_w, odd_a_log, odd_dt_bias, odd_o_gain, odd_w_out, ffn_norm, ffn_w_gate_up, ffn_w_down):
    bsz, seq, _ = x.shape
    x2d = x.reshape(bsz * seq, D_MODEL)
    q, k, v, gb, cu = _inproj0(x2d, seq, even_norm[0], even_w_in[0], even_q_gain[0], even_k_gain[0])
    x2d = _mixer0(x2d, bsz, seq, q, k, v, gb, cu, even_sinks[0], even_conv_w[0], even_w_out[0])
    x2d = _ffn(x2d, ffn_norm[0], ffn_w_gate_up[0], ffn_w_down[0])
    qkv, z, ab = _inproj1(x2d, bsz, seq, odd_norm[0], odd_w_in[0], odd_conv_w[0])
    x2d = _gdn(x2d, bsz, seq, qkv, z, ab, odd_a_log[0], odd_dt_bias[0], odd_o_gain[0], odd_w_out[0])
    x2d = _ffn(x2d, ffn_norm[1], ffn_w_gate_up[1], ffn_w_down[1])
    return x2d.reshape(bsz, seq, D_MODEL)
```

```python
import functools

import jax
import jax.numpy as jnp
from jax import lax
from jax.experimental import pallas as pl
from jax.experimental.pallas import tpu as pltpu

D_MODEL = 1024
D_FF = 2816
EPS = 1e-6

HEAD_DIM = 64
ATTN_HEADS = 8
ATTN_KV_HEADS = 2
ATTN_BLOCK = 128
ROPE_THETA = 10000.0
CONV_CH = 512
Q_W = ATTN_HEADS * HEAD_DIM
KV_W = ATTN_KV_HEADS * HEAD_DIM
DN_HEAD_DIM = 128
DN_HEADS = 8
DN_W = DN_HEADS * DN_HEAD_DIM
DN_CONV_WIDTH = 4
DN_CHUNK = 64

LANES = 128
SUBLANES = 8
MXU_DIM = 256
VMEM_LIMIT_BYTES = 56 * 1024 * 1024

ROW_TILE = 512
FFN_ROWS = 512
FFN_SPLITS = (0, 1536, D_FF)
MIX_ROWS = 512
NEG_BIG = -1e30

BF16 = jnp.bfloat16
F32 = jnp.float32


def _rms_norm_rows(xf, gain):
    ms = jnp.mean(xf * xf, axis=-1, keepdims=True)
    return xf * lax.rsqrt(ms + EPS) * gain


def _sigmoid(v):
    return 1.0 / (1.0 + jnp.exp(-v))


def _silu(v):
    h = 0.5 * v
    return h + h * jnp.tanh(h)


def _dot(a, b):
    return jnp.dot(a, b, preferred_element_type=F32)


def _dot_nt(a, b):
    return lax.dot_general(a, b, (((1,), (1,)), ((), ())), preferred_element_type=F32)


def _dot_tn(a, b):
    return lax.dot_general(a, b, (((0,), (0,)), ((), ())), preferred_element_type=F32)


def _const_spec(shape):
    nd = len(shape)
    return pl.BlockSpec(shape, lambda *_: (0,) * nd, pipeline_mode=pl.Buffered(1))


def _block_ones(n, block):
    r = lax.broadcasted_iota(jnp.int32, (n, n), 0) // block
    c = lax.broadcasted_iota(jnp.int32, (n, n), 1) // block
    return (r == c).astype(BF16)


def _group_sumsq(v, block, scale):
    ones = _block_ones(MXU_DIM, block) * scale
    sq = (v * v).astype(BF16)
    parts = [_dot(sq[:, c:c + MXU_DIM], ones) for c in range(0, v.shape[1], MXU_DIM)]
    return parts[0] if len(parts) == 1 else jnp.concatenate(parts, axis=1)


def _causal_conv(cur, prev, w):
    width = w.shape[0]
    sub = lax.broadcasted_iota(jnp.int32, prev.shape, 0)
    out = w[width - 1:width, :] * cur
    for s in range(1, width):
        rolled = pltpu.roll(cur, s, axis=0)
        head = jnp.where(sub < s, pltpu.roll(prev, s, axis=0), rolled[0:SUBLANES])
        shifted = jnp.concatenate([head, rolled[SUBLANES:]], axis=0)
        out = out + w[width - 1 - s:width - s, :] * shifted
    return out


def _params(sem):
    return pltpu.CompilerParams(dimension_semantics=sem, vmem_limit_bytes=VMEM_LIMIT_BYTES)


def _ffn_rows(xf, gain_ref, wgu_ref, wd_ref):
    xn = _rms_norm_rows(xf, gain_ref[...]).astype(BF16)
    acc = xf
    for lo, hi in zip(FFN_SPLITS[:-1], FFN_SPLITS[1:]):
        g = _dot(xn, wgu_ref[:, lo:hi])
        u = _dot(xn, wgu_ref[:, D_FF + lo:D_FF + hi])
        act = (_silu(g) * u).astype(BF16)
        acc = acc + _dot(act, wd_ref[lo:hi, :])
    return acc


def _ffn_kernel(x_ref, gain_ref, wgu_ref, wd_ref, o_ref):
    o_ref[...] = _ffn_rows(x_ref[...], gain_ref, wgu_ref, wd_ref)


def _ffn(x2d, gain, wgu_layers, wd_layers, layer):
    n_rows = x2d.shape[0]
    row = lambda i: (i, 0)
    this_layer = lambda i: (layer, 0, 0)
    return pl.pallas_call(
        _ffn_kernel,
        grid=(n_rows // FFN_ROWS,),
        in_specs=[
            pl.BlockSpec((FFN_ROWS, D_MODEL), row),
            _const_spec((1, D_MODEL)),
            pl.BlockSpec((None, D_MODEL, 2 * D_FF), this_layer, pipeline_mode=pl.Buffered(1)),
            pl.BlockSpec((None, D_FF, D_MODEL), this_layer, pipeline_mode=pl.Buffered(1)),
        ],
        out_specs=pl.BlockSpec((FFN_ROWS, D_MODEL), row),
        out_shape=jax.ShapeDtypeStruct((n_rows, D_MODEL), F32),
        compiler_params=_params(("parallel",)),
        name="ffn",
    )(x2d, gain.reshape(1, D_MODEL), wgu_layers, wd_layers)


QK_COLS = Q_W + 2 * KV_W


def _inproj0_kernel(x_ref, gain_ref, w_ref, qkgain_ref, cos_ref, sin_ref,
                    q_ref, k_ref, v_ref, gb_ref, cu_ref):
    xn = _rms_norm_rows(x_ref[...], gain_ref[...]).astype(BF16)
    qk = _dot(xn, w_ref[:, :QK_COLS])
    c0 = QK_COLS
    v_ref[...] = _dot(xn, w_ref[:, c0:c0 + 2 * KV_W]).astype(BF16)
    c0 += 2 * KV_W
    ms = _group_sumsq(qk, HEAD_DIM, 1.0 / HEAD_DIM)
    gb_ref[...] = _dot(xn, w_ref[:, c0:c0 + CONV_CH]).astype(BF16)
    c0 += CONV_CH
    qk = qk * lax.rsqrt(ms + EPS) * qkgain_ref[...]
    gc = _dot(xn, w_ref[:, c0:c0 + CONV_CH])
    xin = _dot(xn, w_ref[:, c0 + CONV_CH:c0 + 2 * CONV_CH])
    cu_ref[...] = (gc * xin).astype(BF16)
    reps = QK_COLS // LANES
    cos = jnp.concatenate([cos_ref[...]] * reps, axis=1)
    sin = jnp.concatenate([sin_ref[...]] * reps, axis=1)
    half = HEAD_DIM // 2
    lane = lax.broadcasted_iota(jnp.int32, qk.shape, 1)
    partner = jnp.where((lane % HEAD_DIM) < half,
                        pltpu.roll(qk, QK_COLS - half, axis=1),
                        pltpu.roll(qk, half, axis=1))
    qk = qk * cos + partner * sin
    q_ref[...] = qk[:, :Q_W].astype(BF16)
    k_ref[...] = qk[:, Q_W:].astype(BF16)


def _dup_heads(w, n_heads, dim):
    w3 = w.reshape(w.shape[0], n_heads, 1, dim)
    return jnp.concatenate([w3, w3], axis=2).reshape(w.shape[0], 2 * n_heads * dim)


def _inproj0(x2d, seq, gain, w_in, q_gain, k_gain):
    n_rows = x2d.shape[0]
    wq = w_in[:, :Q_W]
    wk = _dup_heads(w_in[:, Q_W:Q_W + KV_W], ATTN_KV_HEADS, HEAD_DIM)
    wv = _dup_heads(w_in[:, Q_W + KV_W:Q_W + 2 * KV_W], ATTN_KV_HEADS, HEAD_DIM)
    w = jnp.concatenate([wq, wk, wv, w_in[:, Q_W + 2 * KV_W:]], axis=1).astype(BF16)
    n_cols = w.shape[1]
    qkgain = jnp.concatenate([jnp.tile(q_gain * (HEAD_DIM ** -0.5), ATTN_HEADS),
                              jnp.tile(k_gain, 2 * ATTN_KV_HEADS)]).reshape(1, QK_COLS).astype(F32)
    inv_freq = ROPE_THETA ** (-jnp.arange(0, HEAD_DIM, 2, dtype=F32) / HEAD_DIM)
    ang = jnp.arange(seq, dtype=F32)[:, None] * inv_freq[None, :]
    cos, sin = jnp.cos(ang), jnp.sin(ang)
    cos_t = jnp.tile(jnp.concatenate([cos, cos], axis=1), (1, LANES // HEAD_DIM))
    sin_t = jnp.tile(jnp.concatenate([-sin, sin], axis=1), (1, LANES // HEAD_DIM))
    tiles_per_seq = seq // ROW_TILE
    row = lambda i: (i, 0)
    pos = lambda i: (i % tiles_per_seq, 0)
    return pl.pallas_call(
        _inproj0_kernel,
        grid=(n_rows // ROW_TILE,),
        in_specs=[
            pl.BlockSpec((ROW_TILE, D_MODEL), row),
            _const_spec((1, D_MODEL)),
            _const_spec((D_MODEL, n_cols)),
            _const_spec((1, QK_COLS)),
            pl.BlockSpec((ROW_TILE, LANES), pos),
            pl.BlockSpec((ROW_TILE, LANES), pos),
        ],
        out_specs=[
            pl.BlockSpec((ROW_TILE, Q_W), row),
            pl.BlockSpec((ROW_TILE, 2 * KV_W), row),
            pl.BlockSpec((ROW_TILE, 2 * KV_W), row),
            pl.BlockSpec((ROW_TILE, CONV_CH), row),
            pl.BlockSpec((ROW_TILE, CONV_CH), row),
        ],
        out_shape=[
            jax.ShapeDtypeStruct((n_rows, Q_W), BF16),
            jax.ShapeDtypeStruct((n_rows, 2 * KV_W), BF16),
            jax.ShapeDtypeStruct((n_rows, 2 * KV_W), BF16),
            jax.ShapeDtypeStruct((n_rows, CONV_CH), BF16),
            jax.ShapeDtypeStruct((n_rows, CONV_CH), BF16),
        ],
        compiler_params=_params(("parallel",)),
        name="inproj0",
    )(x2d, gain.reshape(1, D_MODEL), w, qkgain, cos_t, sin_t)


CONV0_WIDTH = 3


def _mixer0_kernel(sinks_ref, q_ref, k_ref, v_ref, kp_ref, vp_ref, gb_ref, cu_ref, cup_ref,
                   x_ref, convw_ref, wout_ref, o_ref, y_ref):
    first = pl.program_id(1) == 0
    n_blocks = MIX_ROWS // ATTN_BLOCK
    grp = ATTN_HEADS // ATTN_KV_HEADS
    r = lax.broadcasted_iota(jnp.int32, (ATTN_BLOCK, 2 * ATTN_BLOCK), 0)
    c = lax.broadcasted_iota(jnp.int32, (ATTN_BLOCK, 2 * ATTN_BLOCK), 1)
    band = (c - r >= 1) & (c - r <= ATTN_BLOCK)
    lane = lax.broadcasted_iota(jnp.int32, (ATTN_BLOCK, LANES), 1)
    low_half = lane < HEAD_DIM
    block_rows = [slice(j * ATTN_BLOCK, (j + 1) * ATTN_BLOCK) for j in range(n_blocks)]

    def keys_values(ref, prev_ref, j, kv):
        before = prev_ref[:, kv * LANES:(kv + 1) * LANES] if j == 0 else ref[block_rows[j - 1], kv * LANES:(kv + 1) * LANES]
        return jnp.concatenate([before, ref[block_rows[j], kv * LANES:(kv + 1) * LANES]], axis=0)

    def head_queries(j, head):
        q2 = q_ref[block_rows[j], (head // 2) * LANES:(head // 2 + 1) * LANES]
        keep = low_half if head % 2 == 0 else jnp.logical_not(low_half)
        return jnp.where(keep, q2, jnp.zeros_like(q2))

    first_valid = band & ((c >= ATTN_BLOCK) | jnp.logical_not(first))
    heads = range(ATTN_HEADS)
    for j in range(n_blocks):
        valid = first_valid if j == 0 else band
        kk = [keys_values(k_ref, kp_ref, j, kv) for kv in range(ATTN_KV_HEADS)]
        vv = [keys_values(v_ref, vp_ref, j, kv) for kv in range(ATTN_KV_HEADS)]
        def head_slice(stacked, h):
            g = h % grp
            return stacked[h // grp][g * ATTN_BLOCK:(g + 1) * ATTN_BLOCK, :]

        s_g = [_dot_nt(jnp.concatenate([head_queries(j, kv * grp + g) for g in range(grp)], axis=0), kk[kv])
               for kv in range(ATTN_KV_HEADS)]
        s = [jnp.where(valid, head_slice(s_g, h), NEG_BIG) for h in heads]
        m = [jnp.maximum(jnp.max(s[h], axis=-1, keepdims=True), sinks_ref[h]) for h in heads]
        p = [jnp.exp(s[h] - m[h]) for h in heads]
        denom = [jnp.sum(p[h], axis=-1, keepdims=True) + jnp.exp(sinks_ref[h] - m[h]) for h in heads]
        o_g = [_dot(jnp.concatenate([p[kv * grp + g].astype(BF16) for g in range(grp)], axis=0), vv[kv])
               for kv in range(ATTN_KV_HEADS)]
        o = [head_slice(o_g, h) * (1.0 / denom[h]) for h in heads]
        for pair in range(ATTN_HEADS // 2):
            y_ref[block_rows[j], pair * LANES:(pair + 1) * LANES] = jnp.where(
                low_half, o[2 * pair], o[2 * pair + 1]).astype(BF16)
    prev_rows = cup_ref[...].astype(F32)
    prev_rows = jnp.where(first, jnp.zeros_like(prev_rows), prev_rows)
    conv = _causal_conv(cu_ref[...].astype(F32), prev_rows, convw_ref[...])
    y_ref[:, Q_W:] = (gb_ref[...].astype(F32) * conv).astype(BF16)
    o_ref[...] = x_ref[...] + _dot(y_ref[...], wout_ref[...])


def _mixer0(x2d, bsz, seq, q, k, v, gb, cu, sinks, conv_w, w_out):
    n_rows = x2d.shape[0]
    tiles = seq // MIX_ROWS
    blk_per_tile = MIX_ROWS // ATTN_BLOCK
    sub_per_tile = MIX_ROWS // SUBLANES
    row = lambda b, i: (b * tiles + i, 0)
    prev_blk = lambda b, i: (jnp.maximum((b * tiles + i) * blk_per_tile - 1, 0), 0)
    prev_sub = lambda b, i: (jnp.maximum((b * tiles + i) * sub_per_tile - 1, 0), 0)
    return pl.pallas_call(
        _mixer0_kernel,
        grid=(bsz, tiles),
        in_specs=[
            pl.BlockSpec(memory_space=pltpu.SMEM),
            pl.BlockSpec((MIX_ROWS, Q_W), row),
            pl.BlockSpec((MIX_ROWS, 2 * KV_W), row),
            pl.BlockSpec((MIX_ROWS, 2 * KV_W), row),
            pl.BlockSpec((ATTN_BLOCK, 2 * KV_W), prev_blk),
            pl.BlockSpec((ATTN_BLOCK, 2 * KV_W), prev_blk),
            pl.BlockSpec((MIX_ROWS, CONV_CH), row),
            pl.BlockSpec((MIX_ROWS, CONV_CH), row),
            pl.BlockSpec((SUBLANES, CONV_CH), prev_sub),
            pl.BlockSpec((MIX_ROWS, D_MODEL), row),
            _const_spec((CONV0_WIDTH, CONV_CH)),
            _const_spec((D_MODEL, D_MODEL)),
        ],
        out_specs=pl.BlockSpec((MIX_ROWS, D_MODEL), row),
        out_shape=jax.ShapeDtypeStruct((n_rows, D_MODEL), F32),
        scratch_shapes=[
            pltpu.VMEM((MIX_ROWS, D_MODEL), BF16),
        ],
        compiler_params=_params(("parallel", "arbitrary")),
        name="mixer0",
    )(sinks.astype(F32), q, k, v, k, v, gb, cu, cu, x2d, conv_w.astype(F32), w_out.astype(BF16))


GATE_LANES = LANES


def _inproj1_kernel(x_ref, gain_ref, w_ref, wab_ref, convw_ref, qkv_ref, z_ref, ab_ref, halo_ref):
    @pl.when(pl.program_id(1) == 0)
    def _():
        halo_ref[...] = jnp.zeros_like(halo_ref)

    xn = _rms_norm_rows(x_ref[...], gain_ref[...]).astype(BF16)
    ones = _block_ones(MXU_DIM, DN_HEAD_DIM)
    half_w = 0.5 * convw_ref[...]

    def project(c):
        return _dot(xn, w_ref[:, c * DN_W:(c + 1) * DN_W])

    def finish(c, proj):
        cols = slice(c * DN_W, (c + 1) * DN_W)
        h = _causal_conv(proj, halo_ref[:, cols], half_w[:, cols])
        act = h + h * jnp.tanh(h)
        halo_ref[:, cols] = proj[ROW_TILE - SUBLANES:, :]
        if c < 2:
            scale = float(DN_HEAD_DIM) if c == 0 else 1.0
            summer = ones * scale if c == 0 else ones
            sq = (act * act).astype(BF16)
            ss = jnp.concatenate([_dot(sq[:, c0:c0 + MXU_DIM], summer) for c0 in range(0, DN_W, MXU_DIM)],
                                 axis=1)
            act = act * lax.rsqrt(ss + EPS * scale)
        qkv_ref[:, cols] = act.astype(BF16)

    proj_q = project(0)
    proj_k = project(1)
    finish(0, proj_q)
    proj_v = project(2)
    finish(1, proj_k)
    z_ref[...] = project(3).astype(BF16)
    finish(2, proj_v)
    ab_ref[...] = _dot(xn, wab_ref[...])


def _inproj1(x2d, bsz, seq, gain, w_in, conv_w):
    n_rows = x2d.shape[0]
    w = w_in[:, :4 * DN_W].astype(BF16)
    wab = jnp.pad(w_in[:, 4 * DN_W:], ((0, 0), (0, GATE_LANES - 2 * DN_HEADS))).astype(BF16)
    tiles = seq // ROW_TILE
    row = lambda b, i: (b * tiles + i, 0)
    return pl.pallas_call(
        _inproj1_kernel,
        grid=(bsz, tiles),
        in_specs=[
            pl.BlockSpec((ROW_TILE, D_MODEL), row),
            _const_spec((1, D_MODEL)),
            _const_spec((D_MODEL, 4 * DN_W)),
            _const_spec((D_MODEL, GATE_LANES)),
            _const_spec((DN_CONV_WIDTH, 3 * DN_W)),
        ],
        out_specs=[
            pl.BlockSpec((ROW_TILE, 3 * DN_W), row),
            pl.BlockSpec((ROW_TILE, DN_W), row),
            pl.BlockSpec((ROW_TILE, GATE_LANES), row),
        ],
        out_shape=[
            jax.ShapeDtypeStruct((n_rows, 3 * DN_W), BF16),
            jax.ShapeDtypeStruct((n_rows, DN_W), BF16),
            jax.ShapeDtypeStruct((n_rows, GATE_LANES), F32),
        ],
        scratch_shapes=[
            pltpu.VMEM((SUBLANES, 3 * DN_W), F32),
        ],
        compiler_params=_params(("parallel", "arbitrary")),
        name="inproj1",
    )(x2d, gain.reshape(1, D_MODEL), w, wab, conv_w.astype(F32))


GDN_ROWS = MIX_ROWS
GDN_CHUNKS = GDN_ROWS // DN_CHUNK
GDN_GROUP = 8
SPLIT_TERMS = 3


def _split_bf16(v, terms):
    parts, rem = [], v
    for _ in range(terms):
        p = rem.astype(BF16)
        parts.append(p)
        rem = rem - p.astype(F32)
    return parts


def _softplus(v):
    return jnp.maximum(v, 0.0) + jnp.log(1.0 + jnp.exp(-jnp.abs(v)))


def _gdn_kernel(qkv_ref, z_ref, ab_ref, abt_ref, x_ref, gcol_ref, grow_ref, ogain_ref,
                wout_ref, o_ref, state_ref, gate_ref, o_acc_ref):
    @pl.when(pl.program_id(1) == 0)
    def _():
        state_ref[...] = jnp.zeros_like(state_ref)

    gcoef = gcol_ref[...]
    ab = ab_ref[...]
    gate_ref[:, 0:LANES] = gcoef[0:1, :] * _softplus(ab + gcoef[1:2, :])
    gate_ref[:, LANES:2 * LANES] = _sigmoid(ab)
    rcoef = grow_ref[...]

    ci = lax.broadcasted_iota(jnp.int32, (DN_CHUNK, LANES), 0)
    lane = lax.broadcasted_iota(jnp.int32, (DN_CHUNK, LANES), 1)
    cj = lane % DN_CHUNK
    left = lane < DN_CHUNK
    left_row = left[0:1, :]
    lower = ci >= cj
    strict = ci > cj
    eye = jnp.where(ci == cj, 1.0, 0.0)
    tril = lower[:, :DN_CHUNK].astype(BF16)
    triu2 = (ci <= cj).astype(BF16)
    heads = range(DN_HEADS)
    pairs = range(DN_HEADS // 2)
    zero_k = jnp.zeros((DN_CHUNK, LANES), BF16)
    zero_uw = jnp.zeros((DN_CHUNK, 2 * LANES), BF16)
    zero_s = jnp.zeros((DN_HEAD_DIM, DN_HEAD_DIM), BF16)

    def pair_diag(mat):
        return jnp.concatenate([jnp.where(left, mat, jnp.zeros_like(mat)),
                                jnp.where(left, jnp.zeros_like(mat), mat)], axis=0)

    def group_body(it, carry):
        chunks = [it * GDN_GROUP + j for j in range(GDN_GROUP)]
        rows = [pl.ds(pl.multiple_of(c * DN_CHUNK, DN_CHUNK), DN_CHUNK) for c in chunks]
        items = [(j, h) for j in range(GDN_GROUP) for h in heads]
        gc_col = [sum(_dot(tril, p) for p in _split_bf16(gate_ref[r, 0:LANES], SPLIT_TERMS))
                  for r in rows]
        g_row = [rcoef[:, 0:1] * _softplus(abt_ref[c] + rcoef[:, 1:2]) for c in chunks]
        gc_row = [sum(_dot(p, triu2) for p in _split_bf16(g, SPLIT_TERMS)) for g in g_row]
        beta_c = [gate_ref[r, LANES:2 * LANES] for r in rows]

        def head_cols(j, h, group):
            return qkv_ref[rows[j], group * DN_W + h * LANES:group * DN_W + (h + 1) * LANES]

        q16 = [head_cols(j, h, 0) for j, h in items]
        k16 = [head_cols(j, h, 1) for j, h in items]
        k = [t.astype(F32) for t in k16]
        v = [head_cols(j, h, 2).astype(F32) for j, h in items]
        beta = [beta_c[j][:, h:h + 1] for j, h in items]
        gcc = [gc_col[j][:, DN_HEADS + h:DN_HEADS + h + 1] for j, h in items]
        gc_last = [g[DN_CHUNK - 1:DN_CHUNK, :] for g in gcc]
        n = range(len(items))
        eg = [jnp.exp(g) for g in gcc]
        kb = [k[i] * beta[i] for i in n]
        kb16 = [t.astype(BF16) for t in kb]
        pair_items = [(j, p) for j in range(GDN_GROUP) for p in pairs]
        m = range(len(pair_items))
        ia = [j * DN_HEADS + 2 * p for j, p in pair_items]
        ib = [i + 1 for i in ia]
        gcc2 = [jnp.where(left, gcc[ia[t]], gcc[ib[t]]) for t in m]
        gcr2 = [jnp.where(left_row, gc_row[j][DN_HEADS + 2 * p:DN_HEADS + 2 * p + 1, :],
                          gc_row[j][DN_HEADS + 2 * p + 1:DN_HEADS + 2 * p + 2, :]) for j, p in pair_items]
        gamma = [jnp.exp(jnp.where(lower, gcc2[t] - gcr2[t], -jnp.inf)) for t in m]
        s1 = [_dot_nt(jnp.concatenate([jnp.concatenate([kb16[ia[t]], kb16[ib[t]]], axis=1),
                                       jnp.concatenate([q16[ia[t]], q16[ib[t]]], axis=1)], axis=0),
                      jnp.concatenate([jnp.concatenate([k16[ia[t]], zero_k], axis=1),
                                       jnp.concatenate([zero_k, k16[ib[t]]], axis=1)], axis=0))
              for t in m]
        attn = [(s1[t][DN_CHUNK:] * gamma[t]).astype(BF16) for t in m]
        b = [jnp.where(strict, -(s1[t][:DN_CHUNK] * gamma[t]), 0.0).astype(BF16) for t in m]
        p = [eye + b[t].astype(F32) for t in m]
        b = [_dot(b[t], pair_diag(b[t])).astype(BF16) for t in m]
        power = 2
        while power < DN_CHUNK:
            last = 2 * power >= DN_CHUNK
            lhs = [p[t].astype(BF16) if last else jnp.concatenate([b[t], p[t].astype(BF16)], axis=0)
                   for t in m]
            prod = [_dot(lhs[t], pair_diag(b[t])) for t in m]
            if last:
                p = [p[t] + prod[t] for t in m]
            else:
                b = [prod[t][:DN_CHUNK].astype(BF16) for t in m]
                p = [p[t] + prod[t][DN_CHUNK:] for t in m]
            power *= 2
        t16 = [p[t].astype(BF16) for t in m]
        pair_of = {i: (t, 0) for t, i in enumerate(ia)}
        pair_of.update({i: (t, 1) for t, i in enumerate(ib)})

        def own_rows(i, mat, zero):
            return jnp.concatenate([mat, zero] if pair_of[i][1] == 0 else [zero, mat], axis=0)

        rhs = [jnp.concatenate([v[i] * beta[i], kb[i] * eg[i]], axis=1).astype(BF16) for i in n]
        uw = [_dot(t16[pair_of[i][0]], own_rows(i, rhs[i], zero_uw)).astype(BF16) for i in n]
        kd = [(k[i] * jnp.exp(gc_last[i] - gcc[i])).astype(BF16) for i in n]
        nm = [_dot_tn(kd[i], uw[i]) for i in n]
        oa = [_dot(attn[pair_of[i][0]], own_rows(i, uw[i], zero_uw)) for i in n]
        lhs = [jnp.concatenate([nm[i][:, LANES:], q16[i].astype(F32) * eg[i] - oa[i][:, LANES:]],
                               axis=0).astype(BF16) for i in n]
        decay = [jnp.exp(g) for g in gc_last]
        state = [state_ref[h] for h in heads]
        for j in range(GDN_GROUP):
            new_state = []
            for p_ in pairs:
                a, b_ = 2 * p_, 2 * p_ + 1
                i_a, i_b = j * DN_HEADS + a, j * DN_HEADS + b_
                s_diag = jnp.concatenate(
                    [jnp.concatenate([state[a].astype(BF16), zero_s], axis=1),
                     jnp.concatenate([zero_s, state[b_].astype(BF16)], axis=1)], axis=0)
                res = _dot(jnp.concatenate([lhs[i_a], lhs[i_b]], axis=1), s_diag)
                for h, i, cols in ((a, i_a, slice(0, LANES)), (b_, i_b, slice(LANES, 2 * LANES))):
                    new_state.append(decay[i] * state[h] - res[:DN_HEAD_DIM, cols] + nm[i][:, :LANES])
                    o_acc_ref[rows[j], h * LANES:(h + 1) * LANES] = res[DN_HEAD_DIM:, cols] + oa[i][:, :LANES]
            state = new_state
        for h in heads:
            state_ref[h] = state[h]
        return carry

    lax.fori_loop(0, GDN_CHUNKS // GDN_GROUP, group_body, 0)

    o = o_acc_ref[...]
    ms = _group_sumsq(o, DN_HEAD_DIM, 1.0 / DN_HEAD_DIM)
    y = o * lax.rsqrt(ms + EPS) * ogain_ref[...] * _silu(z_ref[...].astype(F32))
    o_ref[...] = x_ref[...] + _dot(y.astype(BF16), wout_ref[...])


def _gdn(x2d, bsz, seq, qkv, z, ab, a_log, dt_bias, o_gain, w_out):
    n_rows = x2d.shape[0]
    n_chunks = n_rows // DN_CHUNK
    abt = ab[:, :2 * DN_HEADS].reshape(n_chunks, DN_CHUNK, 2 * DN_HEADS).transpose(0, 2, 1)
    neg_a = -jnp.exp(a_log.astype(F32))
    pad_lo = jnp.zeros((DN_HEADS,), F32)
    gcol = jnp.zeros((2, LANES), F32)
    gcol = gcol.at[0, DN_HEADS:2 * DN_HEADS].set(neg_a).at[1, DN_HEADS:2 * DN_HEADS].set(dt_bias.astype(F32))
    grow = jnp.stack([jnp.concatenate([pad_lo, neg_a]), jnp.concatenate([pad_lo, dt_bias.astype(F32)])], axis=1)
    ogain = jnp.tile(o_gain.astype(F32), DN_HEADS).reshape(1, DN_W)
    tiles = seq // GDN_ROWS
    row = lambda b, i: (b * tiles + i, 0)
    return pl.pallas_call(
        _gdn_kernel,
        grid=(bsz, tiles),
        in_specs=[
            pl.BlockSpec((GDN_ROWS, 3 * DN_W), row),
            pl.BlockSpec((GDN_ROWS, DN_W), row),
            pl.BlockSpec((GDN_ROWS, GATE_LANES), row),
            pl.BlockSpec((GDN_CHUNKS, 2 * DN_HEADS, DN_CHUNK), lambda b, i: (b * tiles + i, 0, 0)),
            pl.BlockSpec((GDN_ROWS, D_MODEL), row),
            _const_spec((2, LANES)),
            _const_spec((2 * DN_HEADS, 2)),
            _const_spec((1, DN_W)),
            _const_spec((DN_W, D_MODEL)),
        ],
        out_specs=pl.BlockSpec((GDN_ROWS, D_MODEL), row),
        out_shape=jax.ShapeDtypeStruct((n_rows, D_MODEL), F32),
        scratch_shapes=[
            pltpu.VMEM((DN_HEADS, DN_HEAD_DIM, DN_HEAD_DIM), F32),
            pltpu.VMEM((GDN_ROWS, 2 * LANES), F32),
            pltpu.VMEM((GDN_ROWS, DN_W), F32),
        ],
        compiler_params=_params(("parallel", "arbitrary")),
        name="gdn",
    )(qkv, z, ab, abt, x2d, gcol, grow, ogain, w_out.astype(BF16))


def kernel(x, even_norm, even_w_in, even_q_gain, even_k_gain, even_sinks, even_conv_w, even_w_out, odd_norm, odd_w_in, odd_conv_w, odd_a_log, odd_dt_bias, odd_o_gain, odd_w_out, ffn_norm, ffn_w_gate_up, ffn_w_down):
    bsz, seq, _ = x.shape
    x2d = x.reshape(bsz * seq, D_MODEL)
    q, k, v, gb, cu = _inproj0(x2d, seq, even_norm[0], even_w_in[0], even_q_gain[0], even_k_gain[0])
    x2d = _mixer0(x2d, bsz, seq, q, k, v, gb, cu, even_sinks[0], even_conv_w[0], even_w_out[0])
    wgu_layers = ffn_w_gate_up.astype(BF16)
    wd_layers = ffn_w_down.astype(BF16)
    x2d = _ffn(x2d, ffn_norm[0], wgu_layers, wd_layers, 0)
    qkv, z, ab = _inproj1(x2d, bsz, seq, odd_norm[0], odd_w_in[0], odd_conv_w[0])
    x2d = _gdn(x2d, bsz, seq, qkv, z, ab, odd_a_log[0], odd_dt_bias[0], odd_o_gain[0], odd_w_out[0])
    x2d = _ffn(x2d, ffn_norm[1], wgu_layers, wd_layers, 1)
    return x2d.reshape(bsz, seq, D_MODEL)
```

```python
import functools

import jax
import jax.numpy as jnp
from jax import lax
from jax.experimental import pallas as pl
from jax.experimental.pallas import tpu as pltpu

D_MODEL = 1024
D_FF = 2816
EPS = 1e-6

HEAD_DIM = 64
ATTN_HEADS = 8
ATTN_KV_HEADS = 2
ATTN_BLOCK = 128
ROPE_THETA = 10000.0
CONV_CH = 512
Q_W = ATTN_HEADS * HEAD_DIM
KV_W = ATTN_KV_HEADS * HEAD_DIM
DN_HEAD_DIM = 128
DN_HEADS = 8
DN_W = DN_HEADS * DN_HEAD_DIM
DN_CONV_WIDTH = 4
DN_CHUNK = 64

LANES = 128
SUBLANES = 8
MXU_DIM = 256
VMEM_LIMIT_BYTES = 56 * 1024 * 1024

ROW_TILE = 512
FFN_ROWS = 512
FFN_SPLITS = (0, 1536, D_FF)
MIX_ROWS = 512
NEG_BIG = -1e30

BF16 = jnp.bfloat16
F32 = jnp.float32


def _rms_norm_rows(xf, gain):
    ms = jnp.mean(xf * xf, axis=-1, keepdims=True)
    return xf * lax.rsqrt(ms + EPS) * gain


def _sigmoid(v):
    return 1.0 / (1.0 + jnp.exp(-v))


def _silu(v):
    h = 0.5 * v
    return h + h * jnp.tanh(h)


def _dot(a, b):
    return jnp.dot(a, b, preferred_element_type=F32)


def _dot_nt(a, b):
    return lax.dot_general(a, b, (((1,), (1,)), ((), ())), preferred_element_type=F32)


def _dot_tn(a, b):
    return lax.dot_general(a, b, (((0,), (0,)), ((), ())), preferred_element_type=F32)


def _const_spec(shape):
    nd = len(shape)
    return pl.BlockSpec(shape, lambda *_: (0,) * nd, pipeline_mode=pl.Buffered(1))


def _block_ones(n, block):
    r = lax.broadcasted_iota(jnp.int32, (n, n), 0) // block
    c = lax.broadcasted_iota(jnp.int32, (n, n), 1) // block
    return (r == c).astype(BF16)


def _group_sumsq(v, block, scale):
    ones = _block_ones(MXU_DIM, block) * scale
    sq = (v * v).astype(BF16)
    parts = [_dot(sq[:, c:c + MXU_DIM], ones) for c in range(0, v.shape[1], MXU_DIM)]
    return parts[0] if len(parts) == 1 else jnp.concatenate(parts, axis=1)


def _causal_conv(cur, prev, w):
    width = w.shape[0]
    sub = lax.broadcasted_iota(jnp.int32, prev.shape, 0)
    out = w[width - 1:width, :] * cur
    for s in range(1, width):
        rolled = pltpu.roll(cur, s, axis=0)
        head = jnp.where(sub < s, pltpu.roll(prev, s, axis=0), rolled[0:SUBLANES])
        shifted = jnp.concatenate([head, rolled[SUBLANES:]], axis=0)
        out = out + w[width - 1 - s:width - s, :] * shifted
    return out


def _params(sem):
    return pltpu.CompilerParams(dimension_semantics=sem, vmem_limit_bytes=VMEM_LIMIT_BYTES)


def _ffn_rows(xf, gain_ref, wgu_ref, wd_ref):
    xn = _rms_norm_rows(xf, gain_ref[...]).astype(BF16)
    acc = xf
    for lo, hi in zip(FFN_SPLITS[:-1], FFN_SPLITS[1:]):
        g = _dot(xn, wgu_ref[:, lo:hi])
        u = _dot(xn, wgu_ref[:, D_FF + lo:D_FF + hi])
        act = (_silu(g) * u).astype(BF16)
        acc = acc + _dot(act, wd_ref[lo:hi, :])
    return acc


def _ffn_kernel(x_ref, gain_ref, wgu_ref, wd_ref, o_ref):
    o_ref[...] = _ffn_rows(x_ref[...], gain_ref, wgu_ref, wd_ref)


def _ffn(x2d, gain, wgu_layers, wd_layers, layer):
    n_rows = x2d.shape[0]
    row = lambda i: (i, 0)
    this_layer = lambda i: (layer, 0, 0)
    return pl.pallas_call(
        _ffn_kernel,
        grid=(n_rows // FFN_ROWS,),
        in_specs=[
            pl.BlockSpec((FFN_ROWS, D_MODEL), row),
            _const_spec((1, D_MODEL)),
            pl.BlockSpec((None, D_MODEL, 2 * D_FF), this_layer, pipeline_mode=pl.Buffered(1)),
            pl.BlockSpec((None, D_FF, D_MODEL), this_layer, pipeline_mode=pl.Buffered(1)),
        ],
        out_specs=pl.BlockSpec((FFN_ROWS, D_MODEL), row),
        out_shape=jax.ShapeDtypeStruct((n_rows, D_MODEL), F32),
        compiler_params=_params(("parallel",)),
        name="ffn",
    )(x2d, gain.reshape(1, D_MODEL), wgu_layers, wd_layers)


QK_COLS = Q_W + 2 * KV_W


def _inproj0_kernel(x_ref, gain_ref, w_ref, qkgain_ref, cos_ref, sin_ref,
                    q_ref, k_ref, v_ref, gb_ref, cu_ref):
    xn = _rms_norm_rows(x_ref[...], gain_ref[...]).astype(BF16)
    qk = _dot(xn, w_ref[:, :QK_COLS])
    c0 = QK_COLS
    v_ref[...] = _dot(xn, w_ref[:, c0:c0 + 2 * KV_W]).astype(BF16)
    c0 += 2 * KV_W
    ms = _group_sumsq(qk, HEAD_DIM, 1.0 / HEAD_DIM)
    gb_ref[...] = _dot(xn, w_ref[:, c0:c0 + CONV_CH]).astype(BF16)
    c0 += CONV_CH
    qk = qk * lax.rsqrt(ms + EPS) * qkgain_ref[...]
    gc = _dot(xn, w_ref[:, c0:c0 + CONV_CH])
    xin = _dot(xn, w_ref[:, c0 + CONV_CH:c0 + 2 * CONV_CH])
    cu_ref[...] = (gc * xin).astype(BF16)
    reps = QK_COLS // LANES
    cos = jnp.concatenate([cos_ref[...]] * reps, axis=1)
    sin = jnp.concatenate([sin_ref[...]] * reps, axis=1)
    half = HEAD_DIM // 2
    lane = lax.broadcasted_iota(jnp.int32, qk.shape, 1)
    partner = jnp.where((lane % HEAD_DIM) < half,
                        pltpu.roll(qk, QK_COLS - half, axis=1),
                        pltpu.roll(qk, half, axis=1))
    qk = qk * cos + partner * sin
    q_ref[...] = qk[:, :Q_W].astype(BF16)
    k_ref[...] = qk[:, Q_W:].astype(BF16)


def _dup_heads(w, n_heads, dim):
    w3 = w.reshape(w.shape[0], n_heads, 1, dim)
    return jnp.concatenate([w3, w3], axis=2).reshape(w.shape[0], 2 * n_heads * dim)


def _inproj0(x2d, seq, gain, w_in, q_gain, k_gain):
    n_rows = x2d.shape[0]
    wq = w_in[:, :Q_W]
    wk = _dup_heads(w_in[:, Q_W:Q_W + KV_W], ATTN_KV_HEADS, HEAD_DIM)
    wv = _dup_heads(w_in[:, Q_W + KV_W:Q_W + 2 * KV_W], ATTN_KV_HEADS, HEAD_DIM)
    w = jnp.concatenate([wq, wk, wv, w_in[:, Q_W + 2 * KV_W:]], axis=1).astype(BF16)
    n_cols = w.shape[1]
    qkgain = jnp.concatenate([jnp.tile(q_gain * (HEAD_DIM ** -0.5), ATTN_HEADS),
                              jnp.tile(k_gain, 2 * ATTN_KV_HEADS)]).reshape(1, QK_COLS).astype(F32)
    inv_freq = ROPE_THETA ** (-jnp.arange(0, HEAD_DIM, 2, dtype=F32) / HEAD_DIM)
    ang = jnp.arange(seq, dtype=F32)[:, None] * inv_freq[None, :]
    cos, sin = jnp.cos(ang), jnp.sin(ang)
    cos_t = jnp.tile(jnp.concatenate([cos, cos], axis=1), (1, LANES // HEAD_DIM))
    sin_t = jnp.tile(jnp.concatenate([-sin, sin], axis=1), (1, LANES // HEAD_DIM))
    tiles_per_seq = seq // ROW_TILE
    row = lambda i: (i, 0)
    pos = lambda i: (i % tiles_per_seq, 0)
    return pl.pallas_call(
        _inproj0_kernel,
        grid=(n_rows // ROW_TILE,),
        in_specs=[
            pl.BlockSpec((ROW_TILE, D_MODEL), row),
            _const_spec((1, D_MODEL)),
            _const_spec((D_MODEL, n_cols)),
            _const_spec((1, QK_COLS)),
            pl.BlockSpec((ROW_TILE, LANES), pos),
            pl.BlockSpec((ROW_TILE, LANES), pos),
        ],
        out_specs=[
            pl.BlockSpec((ROW_TILE, Q_W), row),
            pl.BlockSpec((ROW_TILE, 2 * KV_W), row),
            pl.BlockSpec((ROW_TILE, 2 * KV_W), row),
            pl.BlockSpec((ROW_TILE, CONV_CH), row),
            pl.BlockSpec((ROW_TILE, CONV_CH), row),
        ],
        out_shape=[
            jax.ShapeDtypeStruct((n_rows, Q_W), BF16),
            jax.ShapeDtypeStruct((n_rows, 2 * KV_W), BF16),
            jax.ShapeDtypeStruct((n_rows, 2 * KV_W), BF16),
            jax.ShapeDtypeStruct((n_rows, CONV_CH), BF16),
            jax.ShapeDtypeStruct((n_rows, CONV_CH), BF16),
        ],
        compiler_params=_params(("parallel",)),
        name="inproj0",
    )(x2d, gain.reshape(1, D_MODEL), w, qkgain, cos_t, sin_t)


CONV0_WIDTH = 3


def _mixer0_kernel(sinks_ref, q_ref, k_ref, v_ref, kp_ref, vp_ref, gb_ref, cu_ref, cup_ref,
                   x_ref, convw_ref, wout_ref, o_ref, y_ref):
    first = pl.program_id(1) == 0
    n_blocks = MIX_ROWS // ATTN_BLOCK
    grp = ATTN_HEADS // ATTN_KV_HEADS
    r = lax.broadcasted_iota(jnp.int32, (ATTN_BLOCK, 2 * ATTN_BLOCK), 0)
    c = lax.broadcasted_iota(jnp.int32, (ATTN_BLOCK, 2 * ATTN_BLOCK), 1)
    band = (c - r >= 1) & (c - r <= ATTN_BLOCK)
    lane = lax.broadcasted_iota(jnp.int32, (ATTN_BLOCK, LANES), 1)
    low_half = lane < HEAD_DIM
    block_rows = [slice(j * ATTN_BLOCK, (j + 1) * ATTN_BLOCK) for j in range(n_blocks)]

    def keys_values(ref, prev_ref, j, kv):
        before = prev_ref[:, kv * LANES:(kv + 1) * LANES] if j == 0 else ref[block_rows[j - 1], kv * LANES:(kv + 1) * LANES]
        return jnp.concatenate([before, ref[block_rows[j], kv * LANES:(kv + 1) * LANES]], axis=0)

    def head_queries(j, head):
        q2 = q_ref[block_rows[j], (head // 2) * LANES:(head // 2 + 1) * LANES]
        keep = low_half if head % 2 == 0 else jnp.logical_not(low_half)
        return jnp.where(keep, q2, jnp.zeros_like(q2))

    first_valid = band & ((c >= ATTN_BLOCK) | jnp.logical_not(first))
    heads = range(ATTN_HEADS)
    for j in range(n_blocks):
        valid = first_valid if j == 0 else band
        kk = [keys_values(k_ref, kp_ref, j, kv) for kv in range(ATTN_KV_HEADS)]
        vv = [keys_values(v_ref, vp_ref, j, kv) for kv in range(ATTN_KV_HEADS)]
        def head_slice(stacked, h):
            g = h % grp
            return stacked[h // grp][g * ATTN_BLOCK:(g + 1) * ATTN_BLOCK, :]

        s_g = [_dot_nt(jnp.concatenate([head_queries(j, kv * grp + g) for g in range(grp)], axis=0), kk[kv])
               for kv in range(ATTN_KV_HEADS)]
        s = [jnp.where(valid, head_slice(s_g, h), NEG_BIG) for h in heads]
        m = [jnp.maximum(jnp.max(s[h], axis=-1, keepdims=True), sinks_ref[h]) for h in heads]
        p = [jnp.exp(s[h] - m[h]) for h in heads]
        denom = [jnp.sum(p[h], axis=-1, keepdims=True) + jnp.exp(sinks_ref[h] - m[h]) for h in heads]
        o_g = [_dot(jnp.concatenate([p[kv * grp + g].astype(BF16) for g in range(grp)], axis=0), vv[kv])
               for kv in range(ATTN_KV_HEADS)]
        o = [head_slice(o_g, h) * (1.0 / denom[h]) for h in heads]
        for pair in range(ATTN_HEADS // 2):
            y_ref[block_rows[j], pair * LANES:(pair + 1) * LANES] = jnp.where(
                low_half, o[2 * pair], o[2 * pair + 1]).astype(BF16)
    prev_rows = cup_ref[...].astype(F32)
    prev_rows = jnp.where(first, jnp.zeros_like(prev_rows), prev_rows)
    conv = _causal_conv(cu_ref[...].astype(F32), prev_rows, convw_ref[...])
    y_ref[:, Q_W:] = (gb_ref[...].astype(F32) * conv).astype(BF16)
    o_ref[...] = x_ref[...] + _dot(y_ref[...], wout_ref[...])


def _mixer0(x2d, bsz, seq, q, k, v, gb, cu, sinks, conv_w, w_out):
    n_rows = x2d.shape[0]
    tiles = seq // MIX_ROWS
    blk_per_tile = MIX_ROWS // ATTN_BLOCK
    sub_per_tile = MIX_ROWS // SUBLANES
    row = lambda b, i: (b * tiles + i, 0)
    prev_blk = lambda b, i: (jnp.maximum((b * tiles + i) * blk_per_tile - 1, 0), 0)
    prev_sub = lambda b, i: (jnp.maximum((b * tiles + i) * sub_per_tile - 1, 0), 0)
    return pl.pallas_call(
        _mixer0_kernel,
        grid=(bsz, tiles),
        in_specs=[
            pl.BlockSpec(memory_space=pltpu.SMEM),
            pl.BlockSpec((MIX_ROWS, Q_W), row),
            pl.BlockSpec((MIX_ROWS, 2 * KV_W), row),
            pl.BlockSpec((MIX_ROWS, 2 * KV_W), row),
            pl.BlockSpec((ATTN_BLOCK, 2 * KV_W), prev_blk),
            pl.BlockSpec((ATTN_BLOCK, 2 * KV_W), prev_blk),
            pl.BlockSpec((MIX_ROWS, CONV_CH), row),
            pl.BlockSpec((MIX_ROWS, CONV_CH), row),
            pl.BlockSpec((SUBLANES, CONV_CH), prev_sub),
            pl.BlockSpec((MIX_ROWS, D_MODEL), row),
            _const_spec((CONV0_WIDTH, CONV_CH)),
            _const_spec((D_MODEL, D_MODEL)),
        ],
        out_specs=pl.BlockSpec((MIX_ROWS, D_MODEL), row),
        out_shape=jax.ShapeDtypeStruct((n_rows, D_MODEL), F32),
        scratch_shapes=[
            pltpu.VMEM((MIX_ROWS, D_MODEL), BF16),
        ],
        compiler_params=_params(("parallel", "arbitrary")),
        name="mixer0",
    )(sinks.astype(F32), q, k, v, k, v, gb, cu, cu, x2d, conv_w.astype(F32), w_out.astype(BF16))


GATE_LANES = LANES


def _inproj1_kernel(x_ref, gain_ref, w_ref, wab_ref, convw_ref, qkv_ref, z_ref, ab_ref, abt_ref, halo_ref):
    @pl.when(pl.program_id(1) == 0)
    def _():
        halo_ref[...] = jnp.zeros_like(halo_ref)

    xn = _rms_norm_rows(x_ref[...], gain_ref[...]).astype(BF16)
    ones = _block_ones(MXU_DIM, DN_HEAD_DIM)
    half_w = 0.5 * convw_ref[...]

    def project(c):
        return _dot(xn, w_ref[:, c * DN_W:(c + 1) * DN_W])

    def finish(c, proj):
        cols = slice(c * DN_W, (c + 1) * DN_W)
        h = _causal_conv(proj, halo_ref[:, cols], half_w[:, cols])
        act = h + h * jnp.tanh(h)
        halo_ref[:, cols] = proj[ROW_TILE - SUBLANES:, :]
        if c < 2:
            scale = float(DN_HEAD_DIM) if c == 0 else 1.0
            summer = ones * scale if c == 0 else ones
            sq = (act * act).astype(BF16)
            ss = jnp.concatenate([_dot(sq[:, c0:c0 + MXU_DIM], summer) for c0 in range(0, DN_W, MXU_DIM)],
                                 axis=1)
            act = act * lax.rsqrt(ss + EPS * scale)
        qkv_ref[:, cols] = act.astype(BF16)

    proj_q = project(0)
    proj_k = project(1)
    finish(0, proj_q)
    proj_v = project(2)
    finish(1, proj_k)
    z_ref[...] = project(3).astype(BF16)
    finish(2, proj_v)
    ab = _dot(xn, wab_ref[...])
    ab_ref[...] = ab
    abt_ref[...] = ab.T[:2 * DN_HEADS, :]


def _inproj1(x2d, bsz, seq, gain, w_in, conv_w):
    n_rows = x2d.shape[0]
    w = w_in.astype(BF16)
    wab = jnp.pad(w_in[:, 4 * DN_W:], ((0, 0), (0, GATE_LANES - 2 * DN_HEADS))).astype(BF16)
    tiles = seq // ROW_TILE
    row = lambda b, i: (b * tiles + i, 0)
    return pl.pallas_call(
        _inproj1_kernel,
        grid=(bsz, tiles),
        in_specs=[
            pl.BlockSpec((ROW_TILE, D_MODEL), row),
            _const_spec((1, D_MODEL)),
            _const_spec((D_MODEL, 4 * DN_W)),
            _const_spec((D_MODEL, GATE_LANES)),
            _const_spec((DN_CONV_WIDTH, 3 * DN_W)),
        ],
        out_specs=[
            pl.BlockSpec((ROW_TILE, 3 * DN_W), row),
            pl.BlockSpec((ROW_TILE, DN_W), row),
            pl.BlockSpec((ROW_TILE, GATE_LANES), row),
            pl.BlockSpec((2 * DN_HEADS, ROW_TILE), lambda b, i: (0, b * tiles + i)),
        ],
        out_shape=[
            jax.ShapeDtypeStruct((n_rows, 3 * DN_W), BF16),
            jax.ShapeDtypeStruct((n_rows, DN_W), BF16),
            jax.ShapeDtypeStruct((n_rows, GATE_LANES), F32),
            jax.ShapeDtypeStruct((2 * DN_HEADS, n_rows), F32),
        ],
        scratch_shapes=[
            pltpu.VMEM((SUBLANES, 3 * DN_W), F32),
        ],
        compiler_params=_params(("parallel", "arbitrary")),
        name="inproj1",
    )(x2d, gain.reshape(1, D_MODEL), w, wab, conv_w.astype(F32))


GDN_ROWS = MIX_ROWS
GDN_CHUNKS = GDN_ROWS // DN_CHUNK
GDN_GROUP = GDN_CHUNKS
SPLIT_TERMS = 3


def _split_bf16(v, terms):
    parts, rem = [], v
    for _ in range(terms):
        p = rem.astype(BF16)
        parts.append(p)
        rem = rem - p.astype(F32)
    return parts


def _softplus(v):
    return jnp.maximum(v, 0.0) + jnp.log(1.0 + jnp.exp(-jnp.abs(v)))


def _gdn_kernel(qkv_ref, z_ref, ab_ref, abt_ref, x_ref, gcol_ref, grow_ref, ogain_ref,
                wout_ref, o_ref, state_ref, gate_ref, o_acc_ref):
    @pl.when(pl.program_id(1) == 0)
    def _():
        state_ref[...] = jnp.zeros_like(state_ref)

    gcoef = gcol_ref[...]
    ab = ab_ref[...]
    gate_ref[:, 0:LANES] = gcoef[0:1, :] * _softplus(ab + gcoef[1:2, :])
    gate_ref[:, LANES:2 * LANES] = _sigmoid(ab)
    rcoef = grow_ref[...]

    ci = lax.broadcasted_iota(jnp.int32, (DN_CHUNK, LANES), 0)
    lane = lax.broadcasted_iota(jnp.int32, (DN_CHUNK, LANES), 1)
    cj = lane % DN_CHUNK
    left = lane < DN_CHUNK
    left_row = left[0:1, :]
    lower = ci >= cj
    strict = ci > cj
    eye = jnp.where(ci == cj, 1.0, 0.0)
    tril = lower[:, :DN_CHUNK].astype(BF16)
    triu2 = (ci <= cj).astype(BF16)
    heads = range(DN_HEADS)
    pairs = range(DN_HEADS // 2)
    zero_k = jnp.zeros((DN_CHUNK, LANES), BF16)
    zero_uw = jnp.zeros((DN_CHUNK, 2 * LANES), BF16)
    zero_s = jnp.zeros((DN_HEAD_DIM, DN_HEAD_DIM), BF16)

    def pair_diag(mat):
        return jnp.concatenate([jnp.where(left, mat, jnp.zeros_like(mat)),
                                jnp.where(left, jnp.zeros_like(mat), mat)], axis=0)

    def group_body(it, carry):
        chunks = [it * GDN_GROUP + j for j in range(GDN_GROUP)]
        rows = [pl.ds(c * DN_CHUNK, DN_CHUNK) for c in chunks]
        items = [(j, h) for j in range(GDN_GROUP) for h in heads]
        gc_col = [sum(_dot(tril, p) for p in _split_bf16(gate_ref[r, 0:LANES], SPLIT_TERMS))
                  for r in rows]
        g_row = [rcoef[:, 0:1] * _softplus(abt_ref[:, r] + rcoef[:, 1:2]) for r in rows]
        gc_row = [sum(_dot(p, triu2) for p in _split_bf16(g, SPLIT_TERMS)) for g in g_row]
        beta_c = [gate_ref[r, LANES:2 * LANES] for r in rows]

        def head_cols(j, h, group):
            return qkv_ref[rows[j], group * DN_W + h * LANES:group * DN_W + (h + 1) * LANES]

        q16 = [head_cols(j, h, 0) for j, h in items]
        k16 = [head_cols(j, h, 1) for j, h in items]
        k = [t.astype(F32) for t in k16]
        v = [head_cols(j, h, 2).astype(F32) for j, h in items]
        beta = [beta_c[j][:, h:h + 1] for j, h in items]
        gcc = [gc_col[j][:, DN_HEADS + h:DN_HEADS + h + 1] for j, h in items]
        gc_last = [g[DN_CHUNK - 1:DN_CHUNK, :] for g in gcc]
        n = range(len(items))
        eg = [jnp.exp(g) for g in gcc]
        kb = [k[i] * beta[i] for i in n]
        kb16 = [t.astype(BF16) for t in kb]
        pair_items = [(j, p) for j in range(GDN_GROUP) for p in pairs]
        m = range(len(pair_items))
        ia = [j * DN_HEADS + 2 * p for j, p in pair_items]
        ib = [i + 1 for i in ia]
        gcc2 = [jnp.where(left, gcc[ia[t]], gcc[ib[t]]) for t in m]
        gcr2 = [jnp.where(left_row, gc_row[j][DN_HEADS + 2 * p:DN_HEADS + 2 * p + 1, :],
                          gc_row[j][DN_HEADS + 2 * p + 1:DN_HEADS + 2 * p + 2, :]) for j, p in pair_items]
        gamma = [jnp.exp(jnp.where(lower, gcc2[t] - gcr2[t], -jnp.inf)) for t in m]
        s1 = [_dot_nt(jnp.concatenate([jnp.concatenate([kb16[ia[t]], kb16[ib[t]]], axis=1),
                                       jnp.concatenate([q16[ia[t]], q16[ib[t]]], axis=1)], axis=0),
                      jnp.concatenate([jnp.concatenate([k16[ia[t]], zero_k], axis=1),
                                       jnp.concatenate([zero_k, k16[ib[t]]], axis=1)], axis=0))
              for t in m]
        attn = [(s1[t][DN_CHUNK:] * gamma[t]).astype(BF16) for t in m]
        b = [jnp.where(strict, -(s1[t][:DN_CHUNK] * gamma[t]), 0.0).astype(BF16) for t in m]
        p = [eye + b[t].astype(F32) for t in m]
        b = [_dot(b[t], pair_diag(b[t])).astype(BF16) for t in m]
        power = 2
        while power < DN_CHUNK:
            last = 2 * power >= DN_CHUNK
            lhs = [p[t].astype(BF16) if last else jnp.concatenate([b[t], p[t].astype(BF16)], axis=0)
                   for t in m]
            prod = [_dot(lhs[t], pair_diag(b[t])) for t in m]
            if last:
                p = [p[t] + prod[t] for t in m]
            else:
                b = [prod[t][:DN_CHUNK].astype(BF16) for t in m]
                p = [p[t] + prod[t][DN_CHUNK:] for t in m]
            power *= 2
        t16 = [p[t].astype(BF16) for t in m]
        pair_of = {i: (t, 0) for t, i in enumerate(ia)}
        pair_of.update({i: (t, 1) for t, i in enumerate(ib)})

        def own_rows(i, mat, zero):
            return jnp.concatenate([mat, zero] if pair_of[i][1] == 0 else [zero, mat], axis=0)

        rhs = [jnp.concatenate([v[i] * beta[i], kb[i] * eg[i]], axis=1).astype(BF16) for i in n]
        uw = [_dot(t16[pair_of[i][0]], own_rows(i, rhs[i], zero_uw)).astype(BF16) for i in n]
        kd = [(k[i] * jnp.exp(gc_last[i] - gcc[i])).astype(BF16) for i in n]
        nm = [_dot_tn(kd[i], uw[i]) for i in n]
        oa = [_dot(attn[pair_of[i][0]], own_rows(i, uw[i], zero_uw)) for i in n]
        lhs = [jnp.concatenate([nm[i][:, LANES:], q16[i].astype(F32) * eg[i] - oa[i][:, LANES:]],
                               axis=0).astype(BF16) for i in n]
        decay = [jnp.exp(g) for g in gc_last]
        state = [state_ref[h] for h in heads]
        for j in range(GDN_GROUP):
            new_state = []
            for p_ in pairs:
                a, b_ = 2 * p_, 2 * p_ + 1
                i_a, i_b = j * DN_HEADS + a, j * DN_HEADS + b_
                s_diag = jnp.concatenate(
                    [jnp.concatenate([state[a].astype(BF16), zero_s], axis=1),
                     jnp.concatenate([zero_s, state[b_].astype(BF16)], axis=1)], axis=0)
                res = _dot(jnp.concatenate([lhs[i_a], lhs[i_b]], axis=1), s_diag)
                for h, i, cols in ((a, i_a, slice(0, LANES)), (b_, i_b, slice(LANES, 2 * LANES))):
                    new_state.append(decay[i] * state[h] - res[:DN_HEAD_DIM, cols] + nm[i][:, :LANES])
                    o_acc_ref[rows[j], h * LANES:(h + 1) * LANES] = res[DN_HEAD_DIM:, cols] + oa[i][:, :LANES]
            state = new_state
        for h in heads:
            state_ref[h] = state[h]
        return carry

    for it in range(GDN_CHUNKS // GDN_GROUP):
        group_body(it, 0)

    o = o_acc_ref[...]
    ms = _group_sumsq(o, DN_HEAD_DIM, 1.0 / DN_HEAD_DIM)
    y = o * lax.rsqrt(ms + EPS) * ogain_ref[...] * _silu(z_ref[...].astype(F32))
    o_ref[...] = x_ref[...] + _dot(y.astype(BF16), wout_ref[...])


def _gdn(x2d, bsz, seq, qkv, z, ab, abt, a_log, dt_bias, o_gain, w_out):
    n_rows = x2d.shape[0]
    neg_a = -jnp.exp(a_log.astype(F32))
    pad_lo = jnp.zeros((DN_HEADS,), F32)
    gcol = jnp.zeros((2, LANES), F32)
    gcol = gcol.at[0, DN_HEADS:2 * DN_HEADS].set(neg_a).at[1, DN_HEADS:2 * DN_HEADS].set(dt_bias.astype(F32))
    grow = jnp.stack([jnp.concatenate([pad_lo, neg_a]), jnp.concatenate([pad_lo, dt_bias.astype(F32)])], axis=1)
    ogain = jnp.tile(o_gain.astype(F32), DN_HEADS).reshape(1, DN_W)
    tiles = seq // GDN_ROWS
    row = lambda b, i: (b * tiles + i, 0)
    return pl.pallas_call(
        _gdn_kernel,
        grid=(bsz, tiles),
        in_specs=[
            pl.BlockSpec((GDN_ROWS, 3 * DN_W), row),
            pl.BlockSpec((GDN_ROWS, DN_W), row),
            pl.BlockSpec((GDN_ROWS, GATE_LANES), row),
            pl.BlockSpec((2 * DN_HEADS, GDN_ROWS), lambda b, i: (0, b * tiles + i)),
            pl.BlockSpec((GDN_ROWS, D_MODEL), row),
            _const_spec((2, LANES)),
            _const_spec((2 * DN_HEADS, 2)),
            _const_spec((1, DN_W)),
            _const_spec((DN_W, D_MODEL)),
        ],
        out_specs=pl.BlockSpec((GDN_ROWS, D_MODEL), row),
        out_shape=jax.ShapeDtypeStruct((n_rows, D_MODEL), F32),
        scratch_shapes=[
            pltpu.VMEM((DN_HEADS, DN_HEAD_DIM, DN_HEAD_DIM), F32),
            pltpu.VMEM((GDN_ROWS, 2 * LANES), F32),
            pltpu.VMEM((GDN_ROWS, DN_W), F32),
        ],
        compiler_params=_params(("parallel", "arbitrary")),
        name="gdn",
    )(qkv, z, ab, abt, x2d, gcol, grow, ogain, w_out.astype(BF16))


def kernel(x, even_norm, even_w_in, even_q_gain, even_k_gain, even_sinks, even_conv_w, even_w_out, odd_norm, odd_w_in, odd_conv_w, odd_a_log, odd_dt_bias, odd_o_gain, odd_w_out, ffn_norm, ffn_w_gate_up, ffn_w_down):
    bsz, seq, _ = x.shape
    x2d = x.reshape(bsz * seq, D_MODEL)
    q, k, v, gb, cu = _inproj0(x2d, seq, even_norm[0], even_w_in[0], even_q_gain[0], even_k_gain[0])
    x2d = _mixer0(x2d, bsz, seq, q, k, v, gb, cu, even_sinks[0], even_conv_w[0], even_w_out[0])
    wgu_layers = ffn_w_gate_up.astype(BF16)
    wd_layers = ffn_w_down.astype(BF16)
    x2d = _ffn(x2d, ffn_norm[0], wgu_layers, wd_layers, 0)
    qkv, z, ab, abt = _inproj1(x2d, bsz, seq, odd_norm[0], odd_w_in[0], odd_conv_w[0])
    x2d = _gdn(x2d, bsz, seq, qkv, z, ab, abt, odd_a_log[0], odd_dt_bias[0], odd_o_gain[0], odd_w_out[0])
    x2d = _ffn(x2d, ffn_norm[1], wgu_layers, wd_layers, 1)
    return x2d.reshape(bsz, seq, D_MODEL)
```

```python
import functools

import jax
import jax.numpy as jnp
from jax import lax
from jax.experimental import pallas as pl
from jax.experimental.pallas import tpu as pltpu

D_MODEL = 1024
D_FF = 2816
EPS = 1e-6

HEAD_DIM = 64
ATTN_HEADS = 8
ATTN_KV_HEADS = 2
ATTN_BLOCK = 128
ROPE_THETA = 10000.0
CONV_CH = 512
Q_W = ATTN_HEADS * HEAD_DIM
KV_W = ATTN_KV_HEADS * HEAD_DIM
DN_HEAD_DIM = 128
DN_HEADS = 8
DN_W = DN_HEADS * DN_HEAD_DIM
DN_CONV_WIDTH = 4
DN_CHUNK = 64

LANES = 128
SUBLANES = 8
MXU_DIM = 256
VMEM_LIMIT_BYTES = 56 * 1024 * 1024

ROW_TILE = 1024
FFN_ROWS = 512
INPROJ1_ROWS = 1024
FFN_SPLITS = (0, 1536, D_FF)
MIX_ROWS = 1024
NEG_BIG = -1e30

BF16 = jnp.bfloat16
F32 = jnp.float32


def _rms_norm_rows(xf, gain):
    ms = jnp.mean(xf * xf, axis=-1, keepdims=True)
    return xf * lax.rsqrt(ms + EPS) * gain


def _sigmoid(v):
    return 1.0 / (1.0 + jnp.exp(-v))


def _silu(v):
    h = 0.5 * v
    return h + h * jnp.tanh(h)


def _dot(a, b):
    return jnp.dot(a, b, preferred_element_type=F32)


def _dot_nt(a, b):
    return lax.dot_general(a, b, (((1,), (1,)), ((), ())), preferred_element_type=F32)


def _dot_tn(a, b):
    return lax.dot_general(a, b, (((0,), (0,)), ((), ())), preferred_element_type=F32)


def _const_spec(shape):
    nd = len(shape)
    return pl.BlockSpec(shape, lambda *_: (0,) * nd, pipeline_mode=pl.Buffered(1))


def _block_ones(n, block):
    r = lax.broadcasted_iota(jnp.int32, (n, n), 0) // block
    c = lax.broadcasted_iota(jnp.int32, (n, n), 1) // block
    return (r == c).astype(BF16)


def _group_sumsq(v, block, scale):
    ones = _block_ones(MXU_DIM, block) * scale
    sq = (v * v).astype(BF16)
    parts = [_dot(sq[:, c:c + MXU_DIM], ones) for c in range(0, v.shape[1], MXU_DIM)]
    return parts[0] if len(parts) == 1 else jnp.concatenate(parts, axis=1)


def _causal_conv(cur, prev, w):
    width = w.shape[0]
    sub = lax.broadcasted_iota(jnp.int32, prev.shape, 0)

    def shift(a, a_prev, s):
        rolled = pltpu.roll(a, s, axis=0)
        head = jnp.where(sub < s, pltpu.roll(a_prev, s, axis=0), rolled[0:SUBLANES])
        return jnp.concatenate([head, rolled[SUBLANES:]], axis=0)

    def tap(j):
        return w[j:j + 1, :]

    even = tap(width - 1) * cur
    odd = tap(width - 2) * cur if width > 1 else None
    odd_prev = tap(width - 2) * prev if width > 1 else None
    for m in range(1, (width + 1) // 2):
        delayed = shift(cur, prev, 2 * m)
        even = even + tap(width - 1 - 2 * m) * delayed
        if width - 2 - 2 * m >= 0:
            odd = odd + tap(width - 2 - 2 * m) * delayed
            odd_prev = odd_prev + tap(width - 2 - 2 * m) * pltpu.roll(prev, 2 * m, axis=0)
    return even if odd is None else even + shift(odd, odd_prev, 1)


def _params(sem):
    return pltpu.CompilerParams(dimension_semantics=sem, vmem_limit_bytes=VMEM_LIMIT_BYTES)


def _ffn_rows(xf, gain_ref, wgu_ref, wd_ref):
    xn = _rms_norm_rows(xf, gain_ref[...]).astype(BF16)
    acc = xf
    for lo, hi in zip(FFN_SPLITS[:-1], FFN_SPLITS[1:]):
        g = _dot(xn, wgu_ref[:, lo:hi])
        u = _dot(xn, wgu_ref[:, D_FF + lo:D_FF + hi])
        act = (_silu(g) * u).astype(BF16)
        acc = acc + _dot(act, wd_ref[lo:hi, :])
    return acc


def _ffn_kernel(x_ref, gain_ref, wgu_ref, wd_ref, o_ref):
    o_ref[...] = _ffn_rows(x_ref[...], gain_ref, wgu_ref, wd_ref)


def _ffn(x2d, gain, wgu_layers, wd_layers, layer):
    n_rows = x2d.shape[0]
    row = lambda i: (i, 0)
    this_layer = lambda i: (layer, 0, 0)
    return pl.pallas_call(
        _ffn_kernel,
        grid=(n_rows // FFN_ROWS,),
        in_specs=[
            pl.BlockSpec((FFN_ROWS, D_MODEL), row),
            _const_spec((1, D_MODEL)),
            pl.BlockSpec((None, D_MODEL, 2 * D_FF), this_layer, pipeline_mode=pl.Buffered(1)),
            pl.BlockSpec((None, D_FF, D_MODEL), this_layer, pipeline_mode=pl.Buffered(1)),
        ],
        out_specs=pl.BlockSpec((FFN_ROWS, D_MODEL), row),
        out_shape=jax.ShapeDtypeStruct((n_rows, D_MODEL), F32),
        compiler_params=_params(("parallel",)),
        name="ffn",
    )(x2d, gain.reshape(1, D_MODEL), wgu_layers, wd_layers)


QK_COLS = Q_W + 2 * KV_W


def _inproj0_kernel(x_ref, gain_ref, w_ref, qkgain_ref, cos_ref, sin_ref,
                    q_ref, k_ref, v_ref, gb_ref, cu_ref):
    xn = _rms_norm_rows(x_ref[...], gain_ref[...]).astype(BF16)
    qk = _dot(xn, w_ref[:, :QK_COLS])
    c0 = QK_COLS
    v_ref[...] = _dot(xn, w_ref[:, c0:c0 + 2 * KV_W]).astype(BF16)
    c0 += 2 * KV_W
    ms = _group_sumsq(qk, HEAD_DIM, 1.0 / HEAD_DIM)
    gb_ref[...] = _dot(xn, w_ref[:, c0:c0 + CONV_CH]).astype(BF16)
    c0 += CONV_CH
    qk = qk * lax.rsqrt(ms + EPS) * qkgain_ref[...]
    gc = _dot(xn, w_ref[:, c0:c0 + CONV_CH])
    xin = _dot(xn, w_ref[:, c0 + CONV_CH:c0 + 2 * CONV_CH])
    cu_ref[...] = (gc * xin).astype(BF16)
    reps = QK_COLS // LANES
    cos = jnp.concatenate([cos_ref[...]] * reps, axis=1)
    sin = jnp.concatenate([sin_ref[...]] * reps, axis=1)
    half = HEAD_DIM // 2
    lane = lax.broadcasted_iota(jnp.int32, qk.shape, 1)
    partner = jnp.where((lane % HEAD_DIM) < half,
                        pltpu.roll(qk, QK_COLS - half, axis=1),
                        pltpu.roll(qk, half, axis=1))
    qk = qk * cos + partner * sin
    q_ref[...] = qk[:, :Q_W].astype(BF16)
    k_ref[...] = qk[:, Q_W:].astype(BF16)


def _dup_heads(w, n_heads, dim):
    w3 = w.reshape(w.shape[0], n_heads, 1, dim)
    return jnp.concatenate([w3, w3], axis=2).reshape(w.shape[0], 2 * n_heads * dim)


def _inproj0(x2d, seq, gain, w_in, q_gain, k_gain):
    n_rows = x2d.shape[0]
    wq = w_in[:, :Q_W]
    wk = _dup_heads(w_in[:, Q_W:Q_W + KV_W], ATTN_KV_HEADS, HEAD_DIM)
    wv = _dup_heads(w_in[:, Q_W + KV_W:Q_W + 2 * KV_W], ATTN_KV_HEADS, HEAD_DIM)
    w = jnp.concatenate([wq, wk, wv, w_in[:, Q_W + 2 * KV_W:]], axis=1).astype(BF16)
    n_cols = w.shape[1]
    qkgain = jnp.concatenate([jnp.tile(q_gain * (HEAD_DIM ** -0.5), ATTN_HEADS),
                              jnp.tile(k_gain, 2 * ATTN_KV_HEADS)]).reshape(1, QK_COLS).astype(F32)
    inv_freq = ROPE_THETA ** (-jnp.arange(0, HEAD_DIM, 2, dtype=F32) / HEAD_DIM)
    ang = jnp.arange(seq, dtype=F32)[:, None] * inv_freq[None, :]
    cos, sin = jnp.cos(ang), jnp.sin(ang)
    cos_t = jnp.tile(jnp.concatenate([cos, cos], axis=1), (1, LANES // HEAD_DIM))
    sin_t = jnp.tile(jnp.concatenate([-sin, sin], axis=1), (1, LANES // HEAD_DIM))
    tiles_per_seq = seq // ROW_TILE
    row = lambda i: (i, 0)
    pos = lambda i: (i % tiles_per_seq, 0)
    return pl.pallas_call(
        _inproj0_kernel,
        grid=(n_rows // ROW_TILE,),
        in_specs=[
            pl.BlockSpec((ROW_TILE, D_MODEL), row),
            _const_spec((1, D_MODEL)),
            _const_spec((D_MODEL, n_cols)),
            _const_spec((1, QK_COLS)),
            pl.BlockSpec((ROW_TILE, LANES), pos),
            pl.BlockSpec((ROW_TILE, LANES), pos),
        ],
        out_specs=[
            pl.BlockSpec((ROW_TILE, Q_W), row),
            pl.BlockSpec((ROW_TILE, 2 * KV_W), row),
            pl.BlockSpec((ROW_TILE, 2 * KV_W), row),
            pl.BlockSpec((ROW_TILE, CONV_CH), row),
            pl.BlockSpec((ROW_TILE, CONV_CH), row),
        ],
        out_shape=[
            jax.ShapeDtypeStruct((n_rows, Q_W), BF16),
            jax.ShapeDtypeStruct((n_rows, 2 * KV_W), BF16),
            jax.ShapeDtypeStruct((n_rows, 2 * KV_W), BF16),
            jax.ShapeDtypeStruct((n_rows, CONV_CH), BF16),
            jax.ShapeDtypeStruct((n_rows, CONV_CH), BF16),
        ],
        compiler_params=_params(("parallel",)),
        name="inproj0",
    )(x2d, gain.reshape(1, D_MODEL), w, qkgain, cos_t, sin_t)


CONV0_WIDTH = 3


def _mixer0_kernel(sinks_ref, q_ref, k_ref, v_ref, kp_ref, vp_ref, gb_ref, cu_ref, cup_ref,
                   x_ref, convw_ref, wout_ref, o_ref, y_ref):
    first = pl.program_id(1) == 0
    n_blocks = MIX_ROWS // ATTN_BLOCK
    grp = ATTN_HEADS // ATTN_KV_HEADS
    r = lax.broadcasted_iota(jnp.int32, (ATTN_BLOCK, 2 * ATTN_BLOCK), 0)
    c = lax.broadcasted_iota(jnp.int32, (ATTN_BLOCK, 2 * ATTN_BLOCK), 1)
    band = (c - r >= 1) & (c - r <= ATTN_BLOCK)
    lane = lax.broadcasted_iota(jnp.int32, (ATTN_BLOCK, LANES), 1)
    low_half = lane < HEAD_DIM
    block_rows = [slice(j * ATTN_BLOCK, (j + 1) * ATTN_BLOCK) for j in range(n_blocks)]

    def keys_values(ref, prev_ref, j, kv):
        before = prev_ref[:, kv * LANES:(kv + 1) * LANES] if j == 0 else ref[block_rows[j - 1], kv * LANES:(kv + 1) * LANES]
        return jnp.concatenate([before, ref[block_rows[j], kv * LANES:(kv + 1) * LANES]], axis=0)

    def head_queries(j, head):
        q2 = q_ref[block_rows[j], (head // 2) * LANES:(head // 2 + 1) * LANES]
        keep = low_half if head % 2 == 0 else jnp.logical_not(low_half)
        return jnp.where(keep, q2, jnp.zeros_like(q2))

    first_valid = band & ((c >= ATTN_BLOCK) | jnp.logical_not(first))
    heads = range(ATTN_HEADS)
    for j in range(n_blocks):
        valid = first_valid if j == 0 else band
        kk = [keys_values(k_ref, kp_ref, j, kv) for kv in range(ATTN_KV_HEADS)]
        vv = [keys_values(v_ref, vp_ref, j, kv) for kv in range(ATTN_KV_HEADS)]
        def head_slice(stacked, h):
            g = h % grp
            return stacked[h // grp][g * ATTN_BLOCK:(g + 1) * ATTN_BLOCK, :]

        s_g = [_dot_nt(jnp.concatenate([head_queries(j, kv * grp + g) for g in range(grp)], axis=0), kk[kv])
               for kv in range(ATTN_KV_HEADS)]
        s = [jnp.where(valid, head_slice(s_g, h), NEG_BIG) for h in heads]
        m = [jnp.maximum(jnp.max(s[h], axis=-1, keepdims=True), sinks_ref[h]) for h in heads]
        p = [jnp.exp(s[h] - m[h]) for h in heads]
        denom = [jnp.sum(p[h], axis=-1, keepdims=True) + jnp.exp(sinks_ref[h] - m[h]) for h in heads]
        o_g = [_dot(jnp.concatenate([p[kv * grp + g].astype(BF16) for g in range(grp)], axis=0), vv[kv])
               for kv in range(ATTN_KV_HEADS)]
        o = [head_slice(o_g, h) * (1.0 / denom[h]) for h in heads]
        for pair in range(ATTN_HEADS // 2):
            y_ref[block_rows[j], pair * LANES:(pair + 1) * LANES] = jnp.where(
                low_half, o[2 * pair], o[2 * pair + 1]).astype(BF16)
    prev_rows = cup_ref[...].astype(F32)
    prev_rows = jnp.where(first, jnp.zeros_like(prev_rows), prev_rows)
    conv = _causal_conv(cu_ref[...].astype(F32), prev_rows, convw_ref[...])
    y_ref[:, Q_W:] = (gb_ref[...].astype(F32) * conv).astype(BF16)
    o_ref[...] = x_ref[...] + _dot(y_ref[...], wout_ref[...])


def _mixer0(x2d, bsz, seq, q, k, v, gb, cu, sinks, conv_w, w_out):
    n_rows = x2d.shape[0]
    tiles = seq // MIX_ROWS
    blk_per_tile = MIX_ROWS // ATTN_BLOCK
    sub_per_tile = MIX_ROWS // SUBLANES
    row = lambda b, i: (b * tiles + i, 0)
    prev_blk = lambda b, i: (jnp.maximum((b * tiles + i) * blk_per_tile - 1, 0), 0)
    prev_sub = lambda b, i: (jnp.maximum((b * tiles + i) * sub_per_tile - 1, 0), 0)
    return pl.pallas_call(
        _mixer0_kernel,
        grid=(bsz, tiles),
        in_specs=[
            pl.BlockSpec(memory_space=pltpu.SMEM),
            pl.BlockSpec((MIX_ROWS, Q_W), row),
            pl.BlockSpec((MIX_ROWS, 2 * KV_W), row),
            pl.BlockSpec((MIX_ROWS, 2 * KV_W), row),
            pl.BlockSpec((ATTN_BLOCK, 2 * KV_W), prev_blk),
            pl.BlockSpec((ATTN_BLOCK, 2 * KV_W), prev_blk),
            pl.BlockSpec((MIX_ROWS, CONV_CH), row),
            pl.BlockSpec((MIX_ROWS, CONV_CH), row),
            pl.BlockSpec((SUBLANES, CONV_CH), prev_sub),
            pl.BlockSpec((MIX_ROWS, D_MODEL), row),
            _const_spec((CONV0_WIDTH, CONV_CH)),
            _const_spec((D_MODEL, D_MODEL)),
        ],
        out_specs=pl.BlockSpec((MIX_ROWS, D_MODEL), row),
        out_shape=jax.ShapeDtypeStruct((n_rows, D_MODEL), F32),
        scratch_shapes=[
            pltpu.VMEM((MIX_ROWS, D_MODEL), BF16),
        ],
        compiler_params=_params(("parallel", "arbitrary")),
        name="mixer0",
    )(sinks.astype(F32), q, k, v, k, v, gb, cu, cu, x2d, conv_w.astype(F32), w_out.astype(BF16))


GATE_LANES = LANES


def _inproj1_kernel(x_ref, gain_ref, w_ref, wab_ref, convw_ref, qkv_ref, z_ref, ab_ref, abt_ref, halo_ref):
    @pl.when(pl.program_id(1) == 0)
    def _():
        halo_ref[...] = jnp.zeros_like(halo_ref)

    xn = _rms_norm_rows(x_ref[...], gain_ref[...]).astype(BF16)
    ones = _block_ones(MXU_DIM, DN_HEAD_DIM)
    half_w = 0.5 * convw_ref[...]

    def project(c):
        return _dot(xn, w_ref[:, c * DN_W:(c + 1) * DN_W])

    def finish(c, proj):
        cols = slice(c * DN_W, (c + 1) * DN_W)
        h = _causal_conv(proj, halo_ref[:, cols], half_w[:, cols])
        act = h + h * jnp.tanh(h)
        halo_ref[:, cols] = proj[proj.shape[0] - SUBLANES:, :]
        if c < 2:
            scale = float(DN_HEAD_DIM) if c == 0 else 1.0
            summer = ones * scale if c == 0 else ones
            sq = (act * act).astype(BF16)
            ss = jnp.concatenate([_dot(sq[:, c0:c0 + MXU_DIM], summer) for c0 in range(0, DN_W, MXU_DIM)],
                                 axis=1)
            act = act * lax.rsqrt(ss + EPS * scale)
        qkv_ref[:, cols] = act.astype(BF16)

    proj_q = project(0)
    proj_k = project(1)
    finish(0, proj_q)
    proj_v = project(2)
    finish(1, proj_k)
    z_ref[...] = project(3).astype(BF16)
    finish(2, proj_v)
    ab = _dot(xn, wab_ref[...])
    ab_ref[...] = ab
    abt_ref[...] = ab.T[:2 * DN_HEADS, :]


def _inproj1(x2d, bsz, seq, gain, w_in, conv_w):
    n_rows = x2d.shape[0]
    w = w_in.astype(BF16)
    wab = jnp.pad(w_in[:, 4 * DN_W:], ((0, 0), (0, GATE_LANES - 2 * DN_HEADS))).astype(BF16)
    rows = INPROJ1_ROWS
    tiles = seq // rows
    row = lambda b, i: (b * tiles + i, 0)
    return pl.pallas_call(
        _inproj1_kernel,
        grid=(bsz, tiles),
        in_specs=[
            pl.BlockSpec((rows, D_MODEL), row),
            _const_spec((1, D_MODEL)),
            _const_spec((D_MODEL, 4 * DN_W)),
            _const_spec((D_MODEL, GATE_LANES)),
            _const_spec((DN_CONV_WIDTH, 3 * DN_W)),
        ],
        out_specs=[
            pl.BlockSpec((rows, 3 * DN_W), row),
            pl.BlockSpec((rows, DN_W), row),
            pl.BlockSpec((rows, GATE_LANES), row),
            pl.BlockSpec((2 * DN_HEADS, rows), lambda b, i: (0, b * tiles + i)),
        ],
        out_shape=[
            jax.ShapeDtypeStruct((n_rows, 3 * DN_W), BF16),
            jax.ShapeDtypeStruct((n_rows, DN_W), BF16),
            jax.ShapeDtypeStruct((n_rows, GATE_LANES), F32),
            jax.ShapeDtypeStruct((2 * DN_HEADS, n_rows), F32),
        ],
        scratch_shapes=[
            pltpu.VMEM((SUBLANES, 3 * DN_W), F32),
        ],
        compiler_params=_params(("parallel", "arbitrary")),
        name="inproj1",
    )(x2d, gain.reshape(1, D_MODEL), w, wab, conv_w.astype(F32))


GDN_ROWS = 512
GDN_CHUNKS = GDN_ROWS // DN_CHUNK
GDN_GROUP = GDN_CHUNKS
SPLIT_TERMS = 3


def _split_bf16(v, terms):
    parts, rem = [], v
    for _ in range(terms):
        p = rem.astype(BF16)
        parts.append(p)
        rem = rem - p.astype(F32)
    return parts


def _softplus(v):
    return jnp.maximum(v, 0.0) + jnp.log(1.0 + jnp.exp(-jnp.abs(v)))


def _gdn_kernel(qkv_ref, z_ref, ab_ref, abt_ref, x_ref, gcol_ref, grow_ref, ogain_ref,
                wout_ref, o_ref, state_ref, gate_ref, o_acc_ref):
    @pl.when(pl.program_id(1) == 0)
    def _():
        state_ref[...] = jnp.zeros_like(state_ref)

    gcoef = gcol_ref[...]
    ab = ab_ref[...]
    gate_ref[:, 0:LANES] = gcoef[0:1, :] * _softplus(ab + gcoef[1:2, :])
    gate_ref[:, LANES:2 * LANES] = _sigmoid(ab)
    rcoef = grow_ref[...]

    ci = lax.broadcasted_iota(jnp.int32, (DN_CHUNK, LANES), 0)
    lane = lax.broadcasted_iota(jnp.int32, (DN_CHUNK, LANES), 1)
    cj = lane % DN_CHUNK
    left = lane < DN_CHUNK
    left_row = left[0:1, :]
    lower = ci >= cj
    strict = ci > cj
    eye = jnp.where(ci == cj, 1.0, 0.0)
    tril = lower[:, :DN_CHUNK].astype(BF16)
    triu2 = (ci <= cj).astype(BF16)
    heads = range(DN_HEADS)
    pairs = range(DN_HEADS // 2)
    zero_k = jnp.zeros((DN_CHUNK, LANES), BF16)
    zero_uw = jnp.zeros((DN_CHUNK, 2 * LANES), BF16)
    zero_s = jnp.zeros((DN_HEAD_DIM, DN_HEAD_DIM), BF16)

    def pair_diag(mat):
        return jnp.concatenate([jnp.where(left, mat, jnp.zeros_like(mat)),
                                jnp.where(left, jnp.zeros_like(mat), mat)], axis=0)

    def group_body(it, carry):
        chunks = [it * GDN_GROUP + j for j in range(GDN_GROUP)]
        rows = [pl.ds(c * DN_CHUNK, DN_CHUNK) for c in chunks]
        items = [(j, h) for j in range(GDN_GROUP) for h in heads]
        gc_col = [sum(_dot(tril, p) for p in _split_bf16(gate_ref[r, 0:LANES], SPLIT_TERMS))
                  for r in rows]
        g_row = [rcoef[:, 0:1] * _softplus(abt_ref[:, r] + rcoef[:, 1:2]) for r in rows]
        gc_row = [sum(_dot(p, triu2) for p in _split_bf16(g, SPLIT_TERMS)) for g in g_row]
        beta_c = [gate_ref[r, LANES:2 * LANES] for r in rows]

        def head_cols(j, h, group):
            return qkv_ref[rows[j], group * DN_W + h * LANES:group * DN_W + (h + 1) * LANES]

        q16 = [head_cols(j, h, 0) for j, h in items]
        k16 = [head_cols(j, h, 1) for j, h in items]
        k = [t.astype(F32) for t in k16]
        v = [head_cols(j, h, 2).astype(F32) for j, h in items]
        beta = [beta_c[j][:, h:h + 1] for j, h in items]
        gcc = [gc_col[j][:, DN_HEADS + h:DN_HEADS + h + 1] for j, h in items]
        gc_last = [g[DN_CHUNK - 1:DN_CHUNK, :] for g in gcc]
        n = range(len(items))
        eg = [jnp.exp(g) for g in gcc]
        kb = [k[i] * beta[i] for i in n]
        kb16 = [t.astype(BF16) for t in kb]
        pair_items = [(j, p) for j in range(GDN_GROUP) for p in pairs]
        m = range(len(pair_items))
        ia = [j * DN_HEADS + 2 * p for j, p in pair_items]
        ib = [i + 1 for i in ia]
        gcc2 = [jnp.where(left, gcc[ia[t]], gcc[ib[t]]) for t in m]
        gcr2 = [jnp.where(left_row, gc_row[j][DN_HEADS + 2 * p:DN_HEADS + 2 * p + 1, :],
                          gc_row[j][DN_HEADS + 2 * p + 1:DN_HEADS + 2 * p + 2, :]) for j, p in pair_items]
        gamma = [jnp.exp(jnp.where(lower, gcc2[t] - gcr2[t], -jnp.inf)) for t in m]
        s1 = [_dot_nt(jnp.concatenate([jnp.concatenate([kb16[ia[t]], kb16[ib[t]]], axis=1),
                                       jnp.concatenate([q16[ia[t]], q16[ib[t]]], axis=1)], axis=0),
                      jnp.concatenate([jnp.concatenate([k16[ia[t]], zero_k], axis=1),
                                       jnp.concatenate([zero_k, k16[ib[t]]], axis=1)], axis=0))
              for t in m]
        attn = [(s1[t][DN_CHUNK:] * gamma[t]).astype(BF16) for t in m]
        b = [jnp.where(strict, -(s1[t][:DN_CHUNK] * gamma[t]), 0.0).astype(BF16) for t in m]
        p = [eye + b[t].astype(F32) for t in m]
        b = [_dot(b[t], pair_diag(b[t])).astype(BF16) for t in m]
        power = 2
        while power < DN_CHUNK:
            last = 2 * power >= DN_CHUNK
            lhs = [p[t].astype(BF16) if last else jnp.concatenate([b[t], p[t].astype(BF16)], axis=0)
                   for t in m]
            prod = [_dot(lhs[t], pair_diag(b[t])) for t in m]
            if last:
                p = [p[t] + prod[t] for t in m]
            else:
                b = [prod[t][:DN_CHUNK].astype(BF16) for t in m]
                p = [p[t] + prod[t][DN_CHUNK:] for t in m]
            power *= 2
        t16 = [p[t].astype(BF16) for t in m]
        pair_of = {i: (t, 0) for t, i in enumerate(ia)}
        pair_of.update({i: (t, 1) for t, i in enumerate(ib)})

        def own_rows(i, mat, zero):
            return jnp.concatenate([mat, zero] if pair_of[i][1] == 0 else [zero, mat], axis=0)

        rhs = [jnp.concatenate([v[i] * beta[i], kb[i] * eg[i]], axis=1).astype(BF16) for i in n]
        uw = [_dot(t16[pair_of[i][0]], own_rows(i, rhs[i], zero_uw)).astype(BF16) for i in n]
        kd = [(k[i] * jnp.exp(gc_last[i] - gcc[i])).astype(BF16) for i in n]
        nm = [_dot_tn(kd[i], uw[i]) for i in n]
        oa = [_dot(attn[pair_of[i][0]], own_rows(i, uw[i], zero_uw)) for i in n]
        lhs = [jnp.concatenate([nm[i][:, LANES:], q16[i].astype(F32) * eg[i] - oa[i][:, LANES:]],
                               axis=0).astype(BF16) for i in n]
        decay = [jnp.exp(g) for g in gc_last]
        state = [state_ref[h] for h in heads]
        for j in range(GDN_GROUP):
            new_state = []
            for p_ in pairs:
                a, b_ = 2 * p_, 2 * p_ + 1
                i_a, i_b = j * DN_HEADS + a, j * DN_HEADS + b_
                s_diag = jnp.concatenate(
                    [jnp.concatenate([state[a].astype(BF16), zero_s], axis=1),
                     jnp.concatenate([zero_s, state[b_].astype(BF16)], axis=1)], axis=0)
                res = _dot(jnp.concatenate([lhs[i_a], lhs[i_b]], axis=1), s_diag)
                for h, i, cols in ((a, i_a, slice(0, LANES)), (b_, i_b, slice(LANES, 2 * LANES))):
                    new_state.append(decay[i] * state[h] - res[:DN_HEAD_DIM, cols] + nm[i][:, :LANES])
                    o_acc_ref[rows[j], h * LANES:(h + 1) * LANES] = res[DN_HEAD_DIM:, cols] + oa[i][:, :LANES]
            state = new_state
        for h in heads:
            state_ref[h] = state[h]
        return carry

    for it in range(GDN_CHUNKS // GDN_GROUP):
        group_body(it, 0)

    o = o_acc_ref[...]
    ms = _group_sumsq(o, DN_HEAD_DIM, 1.0 / DN_HEAD_DIM)
    y = o * lax.rsqrt(ms + EPS) * ogain_ref[...] * _silu(z_ref[...].astype(F32))
    o_ref[...] = x_ref[...] + _dot(y.astype(BF16), wout_ref[...])


def _gdn(x2d, bsz, seq, qkv, z, ab, abt, a_log, dt_bias, o_gain, w_out):
    n_rows = x2d.shape[0]
    neg_a = -jnp.exp(a_log.astype(F32))
    pad_lo = jnp.zeros((DN_HEADS,), F32)
    gcol = jnp.zeros((2, LANES), F32)
    gcol = gcol.at[0, DN_HEADS:2 * DN_HEADS].set(neg_a).at[1, DN_HEADS:2 * DN_HEADS].set(dt_bias.astype(F32))
    grow = jnp.stack([jnp.concatenate([pad_lo, neg_a]), jnp.concatenate([pad_lo, dt_bias.astype(F32)])], axis=1)
    ogain = jnp.tile(o_gain.astype(F32), DN_HEADS).reshape(1, DN_W)
    tiles = seq // GDN_ROWS
    row = lambda b, i: (b * tiles + i, 0)
    return pl.pallas_call(
        _gdn_kernel,
        grid=(bsz, tiles),
        in_specs=[
            pl.BlockSpec((GDN_ROWS, 3 * DN_W), row),
            pl.BlockSpec((GDN_ROWS, DN_W), row),
            pl.BlockSpec((GDN_ROWS, GATE_LANES), row),
            pl.BlockSpec((2 * DN_HEADS, GDN_ROWS), lambda b, i: (0, b * tiles + i)),
            pl.BlockSpec((GDN_ROWS, D_MODEL), row),
            _const_spec((2, LANES)),
            _const_spec((2 * DN_HEADS, 2)),
            _const_spec((1, DN_W)),
            _const_spec((DN_W, D_MODEL)),
        ],
        out_specs=pl.BlockSpec((GDN_ROWS, D_MODEL), row),
        out_shape=jax.ShapeDtypeStruct((n_rows, D_MODEL), F32),
        scratch_shapes=[
            pltpu.VMEM((DN_HEADS, DN_HEAD_DIM, DN_HEAD_DIM), F32),
            pltpu.VMEM((GDN_ROWS, 2 * LANES), F32),
            pltpu.VMEM((GDN_ROWS, DN_W), F32),
        ],
        compiler_params=_params(("parallel", "arbitrary")),
        name="gdn",
    )(qkv, z, ab, abt, x2d, gcol, grow, ogain, w_out.astype(BF16))


def kernel(x, even_norm, even_w_in, even_q_gain, even_k_gain, even_sinks, even_conv_w, even_w_out, odd_norm, odd_w_in, odd_conv_w, odd_a_log, odd_dt_bias, odd_o_gain, odd_w_out, ffn_norm, ffn_w_gate_up, ffn_w_down):
    bsz, seq, _ = x.shape
    x2d = x.reshape(bsz * seq, D_MODEL)
    q, k, v, gb, cu = _inproj0(x2d, seq, even_norm[0], even_w_in[0], even_q_gain[0], even_k_gain[0])
    x2d = _mixer0(x2d, bsz, seq, q, k, v, gb, cu, even_sinks[0], even_conv_w[0], even_w_out[0])
    wgu_layers = ffn_w_gate_up.astype(BF16)
    wd_layers = ffn_w_down.astype(BF16)
    x2d = _ffn(x2d, ffn_norm[0], wgu_layers, wd_layers, 0)
    qkv, z, ab, abt = _inproj1(x2d, bsz, seq, odd_norm[0], odd_w_in[0], odd_conv_w[0])
    x2d = _gdn(x2d, bsz, seq, qkv, z, ab, abt, odd_a_log[0], odd_dt_bias[0], odd_o_gain[0], odd_w_out[0])
    x2d = _ffn(x2d, ffn_norm[1], wgu_layers, wd_layers, 1)
    return x2d.reshape(bsz, seq, D_MODEL)
```

```python
import functools

import jax
import jax.numpy as jnp
from jax import lax
from jax.experimental import pallas as pl
from jax.experimental.pallas import tpu as pltpu

D_MODEL = 1024
D_FF = 2816
EPS = 1e-6

HEAD_DIM = 64
ATTN_HEADS = 8
ATTN_KV_HEADS = 2
ATTN_BLOCK = 128
ROPE_THETA = 10000.0
CONV_CH = 512
Q_W = ATTN_HEADS * HEAD_DIM
KV_W = ATTN_KV_HEADS * HEAD_DIM
DN_HEAD_DIM = 128
DN_HEADS = 8
DN_W = DN_HEADS * DN_HEAD_DIM
DN_CONV_WIDTH = 4
DN_CHUNK = 64

LANES = 128
SUBLANES = 8
MXU_DIM = 256
VMEM_LIMIT_BYTES = 56 * 1024 * 1024

ROW_TILE = 1024
FFN_ROWS = 1024
INPROJ1_ROWS = 1024
FFN_SPLITS = (0, 1536, D_FF)
MIX_ROWS = 1024
NEG_BIG = -1e30

BF16 = jnp.bfloat16
F32 = jnp.float32


def _rms_norm_rows(xf, gain):
    ms = jnp.mean(xf * xf, axis=-1, keepdims=True)
    return xf * lax.rsqrt(ms + EPS) * gain


def _sigmoid(v):
    return 1.0 / (1.0 + jnp.exp(-v))


def _silu(v):
    h = 0.5 * v
    return h + h * jnp.tanh(h)


def _dot(a, b):
    return jnp.dot(a, b, preferred_element_type=F32)


def _dot_nt(a, b):
    return lax.dot_general(a, b, (((1,), (1,)), ((), ())), preferred_element_type=F32)


def _dot_tn(a, b):
    return lax.dot_general(a, b, (((0,), (0,)), ((), ())), preferred_element_type=F32)


def _const_spec(shape):
    nd = len(shape)
    return pl.BlockSpec(shape, lambda *_: (0,) * nd, pipeline_mode=pl.Buffered(1))


def _block_ones(n, block):
    r = lax.broadcasted_iota(jnp.int32, (n, n), 0) // block
    c = lax.broadcasted_iota(jnp.int32, (n, n), 1) // block
    return (r == c).astype(BF16)


def _group_sumsq(v, block, scale):
    ones = _block_ones(MXU_DIM, block) * scale
    sq = (v * v).astype(BF16)
    parts = [_dot(sq[:, c:c + MXU_DIM], ones) for c in range(0, v.shape[1], MXU_DIM)]
    return parts[0] if len(parts) == 1 else jnp.concatenate(parts, axis=1)


def _causal_conv(cur, prev, w):
    width = w.shape[0]
    sub = lax.broadcasted_iota(jnp.int32, prev.shape, 0)

    def shift(a, a_prev, s):
        rolled = pltpu.roll(a, s, axis=0)
        head = jnp.where(sub < s, pltpu.roll(a_prev, s, axis=0), rolled[0:SUBLANES])
        return jnp.concatenate([head, rolled[SUBLANES:]], axis=0)

    def tap(j):
        return w[j:j + 1, :]

    even = tap(width - 1) * cur
    odd = tap(width - 2) * cur if width > 1 else None
    odd_prev = tap(width - 2) * prev if width > 1 else None
    for m in range(1, (width + 1) // 2):
        delayed = shift(cur, prev, 2 * m)
        even = even + tap(width - 1 - 2 * m) * delayed
        if width - 2 - 2 * m >= 0:
            odd = odd + tap(width - 2 - 2 * m) * delayed
            odd_prev = odd_prev + tap(width - 2 - 2 * m) * pltpu.roll(prev, 2 * m, axis=0)
    return even if odd is None else even + shift(odd, odd_prev, 1)


def _params(sem):
    return pltpu.CompilerParams(dimension_semantics=sem, vmem_limit_bytes=VMEM_LIMIT_BYTES)


def _ffn_rows(xf, gain_ref, wgu_ref, wd_ref):
    xn = _rms_norm_rows(xf, gain_ref[...]).astype(BF16)
    acc = xf
    for lo, hi in zip(FFN_SPLITS[:-1], FFN_SPLITS[1:]):
        g = _dot(xn, wgu_ref[:, lo:hi])
        u = _dot(xn, wgu_ref[:, D_FF + lo:D_FF + hi])
        act = (_silu(g) * u).astype(BF16)
        acc = acc + _dot(act, wd_ref[lo:hi, :])
    return acc


def _ffn_kernel(x_ref, gain_ref, wgu_ref, wd_ref, o_ref):
    o_ref[...] = _ffn_rows(x_ref[...], gain_ref, wgu_ref, wd_ref)


def _ffn(x2d, gain, wgu_layers, wd_layers, layer):
    n_rows = x2d.shape[0]
    row = lambda i: (i, 0)
    this_layer = lambda i: (layer, 0, 0)
    return pl.pallas_call(
        _ffn_kernel,
        grid=(n_rows // FFN_ROWS,),
        in_specs=[
            pl.BlockSpec((FFN_ROWS, D_MODEL), row),
            _const_spec((1, D_MODEL)),
            pl.BlockSpec((None, D_MODEL, 2 * D_FF), this_layer, pipeline_mode=pl.Buffered(1)),
            pl.BlockSpec((None, D_FF, D_MODEL), this_layer, pipeline_mode=pl.Buffered(1)),
        ],
        out_specs=pl.BlockSpec((FFN_ROWS, D_MODEL), row),
        out_shape=jax.ShapeDtypeStruct((n_rows, D_MODEL), F32),
        compiler_params=_params(("parallel",)),
        name="ffn",
    )(x2d, gain.reshape(1, D_MODEL), wgu_layers, wd_layers)


QK_COLS = Q_W + 2 * KV_W


def _inproj0_kernel(x_ref, gain_ref, w_ref, qkgain_ref, cos_ref, sin_ref,
                    q_ref, k_ref, v_ref, gb_ref, cu_ref):
    xn = _rms_norm_rows(x_ref[...], gain_ref[...]).astype(BF16)
    qk = _dot(xn, w_ref[:, :QK_COLS])
    c0 = QK_COLS
    v_ref[...] = _dot(xn, w_ref[:, c0:c0 + 2 * KV_W]).astype(BF16)
    c0 += 2 * KV_W
    ms = _group_sumsq(qk, HEAD_DIM, 1.0 / HEAD_DIM)
    gb_ref[...] = _dot(xn, w_ref[:, c0:c0 + CONV_CH]).astype(BF16)
    c0 += CONV_CH
    qk = qk * lax.rsqrt(ms + EPS) * qkgain_ref[...]
    gc = _dot(xn, w_ref[:, c0:c0 + CONV_CH])
    xin = _dot(xn, w_ref[:, c0 + CONV_CH:c0 + 2 * CONV_CH])
    cu_ref[...] = (gc * xin).astype(BF16)
    reps = QK_COLS // LANES
    cos = jnp.concatenate([cos_ref[...]] * reps, axis=1)
    sin = jnp.concatenate([sin_ref[...]] * reps, axis=1)
    half = HEAD_DIM // 2
    lane = lax.broadcasted_iota(jnp.int32, qk.shape, 1)
    partner = jnp.where((lane % HEAD_DIM) < half,
                        pltpu.roll(qk, QK_COLS - half, axis=1),
                        pltpu.roll(qk, half, axis=1))
    qk = qk * cos + partner * sin
    q_ref[...] = qk[:, :Q_W].astype(BF16)
    k_ref[...] = qk[:, Q_W:].astype(BF16)


def _dup_heads(w, n_heads, dim):
    w3 = w.reshape(w.shape[0], n_heads, 1, dim)
    return jnp.concatenate([w3, w3], axis=2).reshape(w.shape[0], 2 * n_heads * dim)


def _inproj0(x2d, seq, gain, w_in, q_gain, k_gain):
    n_rows = x2d.shape[0]
    wq = w_in[:, :Q_W]
    wk = _dup_heads(w_in[:, Q_W:Q_W + KV_W], ATTN_KV_HEADS, HEAD_DIM)
    wv = _dup_heads(w_in[:, Q_W + KV_W:Q_W + 2 * KV_W], ATTN_KV_HEADS, HEAD_DIM)
    w = jnp.concatenate([wq, wk, wv, w_in[:, Q_W + 2 * KV_W:]], axis=1).astype(BF16)
    n_cols = w.shape[1]
    qkgain = jnp.concatenate([jnp.tile(q_gain * (HEAD_DIM ** -0.5), ATTN_HEADS),
                              jnp.tile(k_gain, 2 * ATTN_KV_HEADS)]).reshape(1, QK_COLS).astype(F32)
    inv_freq = ROPE_THETA ** (-jnp.arange(0, HEAD_DIM, 2, dtype=F32) / HEAD_DIM)
    ang = jnp.arange(seq, dtype=F32)[:, None] * inv_freq[None, :]
    cos, sin = jnp.cos(ang), jnp.sin(ang)
    cos_t = jnp.tile(jnp.concatenate([cos, cos], axis=1), (1, LANES // HEAD_DIM))
    sin_t = jnp.tile(jnp.concatenate([-sin, sin], axis=1), (1, LANES // HEAD_DIM))
    tiles_per_seq = seq // ROW_TILE
    row = lambda i: (i, 0)
    pos = lambda i: (i % tiles_per_seq, 0)
    return pl.pallas_call(
        _inproj0_kernel,
        grid=(n_rows // ROW_TILE,),
        in_specs=[
            pl.BlockSpec((ROW_TILE, D_MODEL), row),
            _const_spec((1, D_MODEL)),
            _const_spec((D_MODEL, n_cols)),
            _const_spec((1, QK_COLS)),
            pl.BlockSpec((ROW_TILE, LANES), pos),
            pl.BlockSpec((ROW_TILE, LANES), pos),
        ],
        out_specs=[
            pl.BlockSpec((ROW_TILE, Q_W), row),
            pl.BlockSpec((ROW_TILE, 2 * KV_W), row),
            pl.BlockSpec((ROW_TILE, 2 * KV_W), row),
            pl.BlockSpec((ROW_TILE, CONV_CH), row),
            pl.BlockSpec((ROW_TILE, CONV_CH), row),
        ],
        out_shape=[
            jax.ShapeDtypeStruct((n_rows, Q_W), BF16),
            jax.ShapeDtypeStruct((n_rows, 2 * KV_W), BF16),
            jax.ShapeDtypeStruct((n_rows, 2 * KV_W), BF16),
            jax.ShapeDtypeStruct((n_rows, CONV_CH), BF16),
            jax.ShapeDtypeStruct((n_rows, CONV_CH), BF16),
        ],
        compiler_params=_params(("parallel",)),
        name="inproj0",
    )(x2d, gain.reshape(1, D_MODEL), w, qkgain, cos_t, sin_t)


CONV0_WIDTH = 3


def _mixer0_kernel(sinks_ref, q_ref, k_ref, v_ref, kp_ref, vp_ref, gb_ref, cu_ref, cup_ref,
                   x_ref, convw_ref, wout_ref, o_ref, y_ref):
    first = pl.program_id(1) == 0
    n_blocks = MIX_ROWS // ATTN_BLOCK
    grp = ATTN_HEADS // ATTN_KV_HEADS
    r = lax.broadcasted_iota(jnp.int32, (ATTN_BLOCK, 2 * ATTN_BLOCK), 0)
    c = lax.broadcasted_iota(jnp.int32, (ATTN_BLOCK, 2 * ATTN_BLOCK), 1)
    band = (c - r >= 1) & (c - r <= ATTN_BLOCK)
    lane = lax.broadcasted_iota(jnp.int32, (ATTN_BLOCK, LANES), 1)
    low_half = lane < HEAD_DIM
    block_rows = [slice(j * ATTN_BLOCK, (j + 1) * ATTN_BLOCK) for j in range(n_blocks)]

    def keys_values(ref, prev_ref, j, kv):
        before = prev_ref[:, kv * LANES:(kv + 1) * LANES] if j == 0 else ref[block_rows[j - 1], kv * LANES:(kv + 1) * LANES]
        return jnp.concatenate([before, ref[block_rows[j], kv * LANES:(kv + 1) * LANES]], axis=0)

    def head_queries(j, head):
        q2 = q_ref[block_rows[j], (head // 2) * LANES:(head // 2 + 1) * LANES]
        keep = low_half if head % 2 == 0 else jnp.logical_not(low_half)
        return jnp.where(keep, q2, jnp.zeros_like(q2))

    first_valid = band & ((c >= ATTN_BLOCK) | jnp.logical_not(first))
    heads = range(ATTN_HEADS)
    for j in range(n_blocks):
        valid = first_valid if j == 0 else band
        kk = [keys_values(k_ref, kp_ref, j, kv) for kv in range(ATTN_KV_HEADS)]
        vv = [keys_values(v_ref, vp_ref, j, kv) for kv in range(ATTN_KV_HEADS)]
        def head_slice(stacked, h):
            g = h % grp
            return stacked[h // grp][g * ATTN_BLOCK:(g + 1) * ATTN_BLOCK, :]

        s_g = [_dot_nt(jnp.concatenate([head_queries(j, kv * grp + g) for g in range(grp)], axis=0), kk[kv])
               for kv in range(ATTN_KV_HEADS)]
        s = [jnp.where(valid, head_slice(s_g, h), NEG_BIG) for h in heads]
        m = [jnp.maximum(jnp.max(s[h], axis=-1, keepdims=True), sinks_ref[h]) for h in heads]
        p = [jnp.exp(s[h] - m[h]) for h in heads]
        denom = [jnp.sum(p[h], axis=-1, keepdims=True) + jnp.exp(sinks_ref[h] - m[h]) for h in heads]
        o_g = [_dot(jnp.concatenate([p[kv * grp + g].astype(BF16) for g in range(grp)], axis=0), vv[kv])
               for kv in range(ATTN_KV_HEADS)]
        o = [head_slice(o_g, h) * (1.0 / denom[h]) for h in heads]
        for pair in range(ATTN_HEADS // 2):
            y_ref[block_rows[j], pair * LANES:(pair + 1) * LANES] = jnp.where(
                low_half, o[2 * pair], o[2 * pair + 1]).astype(BF16)
    prev_rows = cup_ref[...].astype(F32)
    prev_rows = jnp.where(first, jnp.zeros_like(prev_rows), prev_rows)
    conv = _causal_conv(cu_ref[...].astype(F32), prev_rows, convw_ref[...])
    y_ref[:, Q_W:] = (gb_ref[...].astype(F32) * conv).astype(BF16)
    o_ref[...] = x_ref[...] + _dot(y_ref[...], wout_ref[...])


def _mixer0(x2d, bsz, seq, q, k, v, gb, cu, sinks, conv_w, w_out):
    n_rows = x2d.shape[0]
    tiles = seq // MIX_ROWS
    blk_per_tile = MIX_ROWS // ATTN_BLOCK
    sub_per_tile = MIX_ROWS // SUBLANES
    row = lambda b, i: (b * tiles + i, 0)
    prev_blk = lambda b, i: (jnp.maximum((b * tiles + i) * blk_per_tile - 1, 0), 0)
    prev_sub = lambda b, i: (jnp.maximum((b * tiles + i) * sub_per_tile - 1, 0), 0)
    return pl.pallas_call(
        _mixer0_kernel,
        grid=(bsz, tiles),
        in_specs=[
            pl.BlockSpec(memory_space=pltpu.SMEM),
            pl.BlockSpec((MIX_ROWS, Q_W), row),
            pl.BlockSpec((MIX_ROWS, 2 * KV_W), row),
            pl.BlockSpec((MIX_ROWS, 2 * KV_W), row),
            pl.BlockSpec((ATTN_BLOCK, 2 * KV_W), prev_blk),
            pl.BlockSpec((ATTN_BLOCK, 2 * KV_W), prev_blk),
            pl.BlockSpec((MIX_ROWS, CONV_CH), row),
            pl.BlockSpec((MIX_ROWS, CONV_CH), row),
            pl.BlockSpec((SUBLANES, CONV_CH), prev_sub),
            pl.BlockSpec((MIX_ROWS, D_MODEL), row),
            _const_spec((CONV0_WIDTH, CONV_CH)),
            _const_spec((D_MODEL, D_MODEL)),
        ],
        out_specs=pl.BlockSpec((MIX_ROWS, D_MODEL), row),
        out_shape=jax.ShapeDtypeStruct((n_rows, D_MODEL), F32),
        scratch_shapes=[
            pltpu.VMEM((MIX_ROWS, D_MODEL), BF16),
        ],
        compiler_params=_params(("parallel", "arbitrary")),
        name="mixer0",
    )(sinks.astype(F32), q, k, v, k, v, gb, cu, cu, x2d, conv_w.astype(F32), w_out.astype(BF16))


GATE_LANES = LANES


def _inproj1_kernel(x_ref, gain_ref, w_ref, wab_ref, convw_ref, qkv_ref, z_ref, ab_ref, abt_ref, halo_ref):
    @pl.when(pl.program_id(1) == 0)
    def _():
        halo_ref[...] = jnp.zeros_like(halo_ref)

    xn = _rms_norm_rows(x_ref[...], gain_ref[...]).astype(BF16)
    ones = _block_ones(MXU_DIM, DN_HEAD_DIM)
    half_w = 0.5 * convw_ref[...]

    def project(c):
        return _dot(xn, w_ref[:, c * DN_W:(c + 1) * DN_W])

    def finish(c, proj):
        cols = slice(c * DN_W, (c + 1) * DN_W)
        h = _causal_conv(proj, halo_ref[:, cols], half_w[:, cols])
        act = h + h * jnp.tanh(h)
        halo_ref[:, cols] = proj[proj.shape[0] - SUBLANES:, :]
        if c < 2:
            scale = float(DN_HEAD_DIM) if c == 0 else 1.0
            summer = ones * scale if c == 0 else ones
            sq = (act * act).astype(BF16)
            ss = jnp.concatenate([_dot(sq[:, c0:c0 + MXU_DIM], summer) for c0 in range(0, DN_W, MXU_DIM)],
                                 axis=1)
            act = act * lax.rsqrt(ss + EPS * scale)
        qkv_ref[:, cols] = act.astype(BF16)

    proj_q = project(0)
    proj_k = project(1)
    finish(0, proj_q)
    proj_v = project(2)
    finish(1, proj_k)
    z_ref[...] = project(3).astype(BF16)
    finish(2, proj_v)
    ab = _dot(xn, wab_ref[...])
    ab_ref[...] = ab
    abt_ref[...] = ab.T[:2 * DN_HEADS, :]


def _inproj1(x2d, bsz, seq, gain, w_in, conv_w):
    n_rows = x2d.shape[0]
    w = w_in.astype(BF16)
    wab = jnp.pad(w_in[:, 4 * DN_W:], ((0, 0), (0, GATE_LANES - 2 * DN_HEADS))).astype(BF16)
    rows = INPROJ1_ROWS
    tiles = seq // rows
    row = lambda b, i: (b * tiles + i, 0)
    return pl.pallas_call(
        _inproj1_kernel,
        grid=(bsz, tiles),
        in_specs=[
            pl.BlockSpec((rows, D_MODEL), row),
            _const_spec((1, D_MODEL)),
            _const_spec((D_MODEL, 4 * DN_W)),
            _const_spec((D_MODEL, GATE_LANES)),
            _const_spec((DN_CONV_WIDTH, 3 * DN_W)),
        ],
        out_specs=[
            pl.BlockSpec((rows, 3 * DN_W), row),
            pl.BlockSpec((rows, DN_W), row),
            pl.BlockSpec((rows, GATE_LANES), row),
            pl.BlockSpec((2 * DN_HEADS, rows), lambda b, i: (0, b * tiles + i)),
        ],
        out_shape=[
            jax.ShapeDtypeStruct((n_rows, 3 * DN_W), BF16),
            jax.ShapeDtypeStruct((n_rows, DN_W), BF16),
            jax.ShapeDtypeStruct((n_rows, GATE_LANES), F32),
            jax.ShapeDtypeStruct((2 * DN_HEADS, n_rows), F32),
        ],
        scratch_shapes=[
            pltpu.VMEM((SUBLANES, 3 * DN_W), F32),
        ],
        compiler_params=_params(("parallel", "arbitrary")),
        name="inproj1",
    )(x2d, gain.reshape(1, D_MODEL), w, wab, conv_w.astype(F32))


GDN_ROWS = 1024
GDN_CHUNKS = GDN_ROWS // DN_CHUNK
GDN_GROUP = 8
SPLIT_TERMS = 3


def _split_bf16(v, terms):
    parts, rem = [], v
    for _ in range(terms):
        p = rem.astype(BF16)
        parts.append(p)
        rem = rem - p.astype(F32)
    return parts


def _softplus(v):
    return jnp.maximum(v, 0.0) + jnp.log(1.0 + jnp.exp(-jnp.abs(v)))


def _gdn_kernel(qkv_ref, z_ref, ab_ref, abt_ref, x_ref, gcol_ref, grow_ref, ogain_ref,
                wout_ref, o_ref, state_ref, gate_ref, o_acc_ref):
    @pl.when(pl.program_id(1) == 0)
    def _():
        state_ref[...] = jnp.zeros_like(state_ref)

    gcoef = gcol_ref[...]
    ab = ab_ref[...]
    gate_ref[:, 0:LANES] = gcoef[0:1, :] * _softplus(ab + gcoef[1:2, :])
    gate_ref[:, LANES:2 * LANES] = _sigmoid(ab)
    rcoef = grow_ref[...]

    ci = lax.broadcasted_iota(jnp.int32, (DN_CHUNK, LANES), 0)
    lane = lax.broadcasted_iota(jnp.int32, (DN_CHUNK, LANES), 1)
    cj = lane % DN_CHUNK
    left = lane < DN_CHUNK
    left_row = left[0:1, :]
    lower = ci >= cj
    strict = ci > cj
    eye = jnp.where(ci == cj, 1.0, 0.0)
    tril = lower[:, :DN_CHUNK].astype(BF16)
    triu2 = (ci <= cj).astype(BF16)
    heads = range(DN_HEADS)
    pairs = range(DN_HEADS // 2)
    zero_k = jnp.zeros((DN_CHUNK, LANES), BF16)
    zero_uw = jnp.zeros((DN_CHUNK, 2 * LANES), BF16)
    zero_s = jnp.zeros((DN_HEAD_DIM, DN_HEAD_DIM), BF16)

    def pair_diag(mat):
        return jnp.concatenate([jnp.where(left, mat, jnp.zeros_like(mat)),
                                jnp.where(left, jnp.zeros_like(mat), mat)], axis=0)

    def group_body(it, carry):
        chunks = [it * GDN_GROUP + j for j in range(GDN_GROUP)]
        rows = [pl.ds(c * DN_CHUNK, DN_CHUNK) for c in chunks]
        items = [(j, h) for j in range(GDN_GROUP) for h in heads]
        gc_col = [sum(_dot(tril, p) for p in _split_bf16(gate_ref[r, 0:LANES], SPLIT_TERMS))
                  for r in rows]
        g_row = [rcoef[:, 0:1] * _softplus(abt_ref[:, r] + rcoef[:, 1:2]) for r in rows]
        gc_row = [sum(_dot(p, triu2) for p in _split_bf16(g, SPLIT_TERMS)) for g in g_row]
        beta_c = [gate_ref[r, LANES:2 * LANES] for r in rows]

        def head_cols(j, h, group):
            return qkv_ref[rows[j], group * DN_W + h * LANES:group * DN_W + (h + 1) * LANES]

        q16 = [head_cols(j, h, 0) for j, h in items]
        k16 = [head_cols(j, h, 1) for j, h in items]
        k = [t.astype(F32) for t in k16]
        v = [head_cols(j, h, 2).astype(F32) for j, h in items]
        beta = [beta_c[j][:, h:h + 1] for j, h in items]
        gcc = [gc_col[j][:, DN_HEADS + h:DN_HEADS + h + 1] for j, h in items]
        gc_last = [g[DN_CHUNK - 1:DN_CHUNK, :] for g in gcc]
        n = range(len(items))
        eg = [jnp.exp(g) for g in gcc]
        kb = [k[i] * beta[i] for i in n]
        kb16 = [t.astype(BF16) for t in kb]
        pair_items = [(j, p) for j in range(GDN_GROUP) for p in pairs]
        m = range(len(pair_items))
        ia = [j * DN_HEADS + 2 * p for j, p in pair_items]
        ib = [i + 1 for i in ia]
        gcc2 = [jnp.where(left, gcc[ia[t]], gcc[ib[t]]) for t in m]
        gcr2 = [jnp.where(left_row, gc_row[j][DN_HEADS + 2 * p:DN_HEADS + 2 * p + 1, :],
                          gc_row[j][DN_HEADS + 2 * p + 1:DN_HEADS + 2 * p + 2, :]) for j, p in pair_items]
        gamma = [jnp.exp(jnp.where(lower, gcc2[t] - gcr2[t], -jnp.inf)) for t in m]
        s1 = [_dot_nt(jnp.concatenate([jnp.concatenate([kb16[ia[t]], kb16[ib[t]]], axis=1),
                                       jnp.concatenate([q16[ia[t]], q16[ib[t]]], axis=1)], axis=0),
                      jnp.concatenate([jnp.concatenate([k16[ia[t]], zero_k], axis=1),
                                       jnp.concatenate([zero_k, k16[ib[t]]], axis=1)], axis=0))
              for t in m]
        attn = [(s1[t][DN_CHUNK:] * gamma[t]).astype(BF16) for t in m]
        b = [jnp.where(strict, -(s1[t][:DN_CHUNK] * gamma[t]), 0.0).astype(BF16) for t in m]
        p = [eye + b[t].astype(F32) for t in m]
        b = [_dot(b[t], pair_diag(b[t])).astype(BF16) for t in m]
        power = 2
        while power < DN_CHUNK:
            last = 2 * power >= DN_CHUNK
            lhs = [p[t].astype(BF16) if last else jnp.concatenate([b[t], p[t].astype(BF16)], axis=0)
                   for t in m]
            prod = [_dot(lhs[t], pair_diag(b[t])) for t in m]
            if last:
                p = [p[t] + prod[t] for t in m]
            else:
                b = [prod[t][:DN_CHUNK].astype(BF16) for t in m]
                p = [p[t] + prod[t][DN_CHUNK:] for t in m]
            power *= 2
        t16 = [p[t].astype(BF16) for t in m]
        pair_of = {i: (t, 0) for t, i in enumerate(ia)}
        pair_of.update({i: (t, 1) for t, i in enumerate(ib)})

        def own_rows(i, mat, zero):
            return jnp.concatenate([mat, zero] if pair_of[i][1] == 0 else [zero, mat], axis=0)

        rhs = [jnp.concatenate([v[i] * beta[i], kb[i] * eg[i]], axis=1).astype(BF16) for i in n]
        uw = [_dot(t16[pair_of[i][0]], own_rows(i, rhs[i], zero_uw)).astype(BF16) for i in n]
        kd = [(k[i] * jnp.exp(gc_last[i] - gcc[i])).astype(BF16) for i in n]
        nm = [_dot_tn(kd[i], uw[i]) for i in n]
        oa = [_dot(attn[pair_of[i][0]], own_rows(i, uw[i], zero_uw)) for i in n]
        lhs = [jnp.concatenate([nm[i][:, LANES:], q16[i].astype(F32) * eg[i] - oa[i][:, LANES:]],
                               axis=0).astype(BF16) for i in n]
        decay = [jnp.exp(g) for g in gc_last]
        state = [state_ref[h] for h in heads]
        for j in range(GDN_GROUP):
            new_state = []
            for p_ in pairs:
                a, b_ = 2 * p_, 2 * p_ + 1
                i_a, i_b = j * DN_HEADS + a, j * DN_HEADS + b_
                s_diag = jnp.concatenate(
                    [jnp.concatenate([state[a].astype(BF16), zero_s], axis=1),
                     jnp.concatenate([zero_s, state[b_].astype(BF16)], axis=1)], axis=0)
                res = _dot(jnp.concatenate([lhs[i_a], lhs[i_b]], axis=1), s_diag)
                for h, i, cols in ((a, i_a, slice(0, LANES)), (b_, i_b, slice(LANES, 2 * LANES))):
                    new_state.append(decay[i] * state[h] - res[:DN_HEAD_DIM, cols] + nm[i][:, :LANES])
                    o_acc_ref[rows[j], h * LANES:(h + 1) * LANES] = res[DN_HEAD_DIM:, cols] + oa[i][:, :LANES]
            state = new_state
        for h in heads:
            state_ref[h] = state[h]
        return carry

    for it in range(GDN_CHUNKS // GDN_GROUP):
        group_body(it, 0)

    o = o_acc_ref[...]
    ms = _group_sumsq(o, DN_HEAD_DIM, 1.0 / DN_HEAD_DIM)
    y = o * lax.rsqrt(ms + EPS) * ogain_ref[...] * _silu(z_ref[...].astype(F32))
    o_ref[...] = x_ref[...] + _dot(y.astype(BF16), wout_ref[...])


def _gdn(x2d, bsz, seq, qkv, z, ab, abt, a_log, dt_bias, o_gain, w_out):
    n_rows = x2d.shape[0]
    neg_a = -jnp.exp(a_log.astype(F32))
    pad_lo = jnp.zeros((DN_HEADS,), F32)
    gcol = jnp.zeros((2, LANES), F32)
    gcol = gcol.at[0, DN_HEADS:2 * DN_HEADS].set(neg_a).at[1, DN_HEADS:2 * DN_HEADS].set(dt_bias.astype(F32))
    grow = jnp.stack([jnp.concatenate([pad_lo, neg_a]), jnp.concatenate([pad_lo, dt_bias.astype(F32)])], axis=1)
    ogain = jnp.tile(o_gain.astype(F32), DN_HEADS).reshape(1, DN_W)
    tiles = seq // GDN_ROWS
    row = lambda b, i: (b * tiles + i, 0)
    return pl.pallas_call(
        _gdn_kernel,
        grid=(bsz, tiles),
        in_specs=[
            pl.BlockSpec((GDN_ROWS, 3 * DN_W), row),
            pl.BlockSpec((GDN_ROWS, DN_W), row),
            pl.BlockSpec((GDN_ROWS, GATE_LANES), row),
            pl.BlockSpec((2 * DN_HEADS, GDN_ROWS), lambda b, i: (0, b * tiles + i)),
            pl.BlockSpec((GDN_ROWS, D_MODEL), row),
            _const_spec((2, LANES)),
            _const_spec((2 * DN_HEADS, 2)),
            _const_spec((1, DN_W)),
            _const_spec((DN_W, D_MODEL)),
        ],
        out_specs=pl.BlockSpec((GDN_ROWS, D_MODEL), row),
        out_shape=jax.ShapeDtypeStruct((n_rows, D_MODEL), F32),
        scratch_shapes=[
            pltpu.VMEM((DN_HEADS, DN_HEAD_DIM, DN_HEAD_DIM), F32),
            pltpu.VMEM((GDN_ROWS, 2 * LANES), F32),
            pltpu.VMEM((GDN_ROWS, DN_W), F32),
        ],
        compiler_params=_params(("parallel", "arbitrary")),
        name="gdn",
    )(qkv, z, ab, abt, x2d, gcol, grow, ogain, w_out.astype(BF16))


def kernel(x, even_norm, even_w_in, even_q_gain, even_k_gain, even_sinks, even_conv_w, even_w_out, odd_norm, odd_w_in, odd_conv_w, odd_a_log, odd_dt_bias, odd_o_gain, odd_w_out, ffn_norm, ffn_w_gate_up, ffn_w_down):
    bsz, seq, _ = x.shape
    x2d = x.reshape(bsz * seq, D_MODEL)
    q, k, v, gb, cu = _inproj0(x2d, seq, even_norm[0], even_w_in[0], even_q_gain[0], even_k_gain[0])
    x2d = _mixer0(x2d, bsz, seq, q, k, v, gb, cu, even_sinks[0], even_conv_w[0], even_w_out[0])
    wgu_layers = ffn_w_gate_up.astype(BF16)
    wd_layers = ffn_w_down.astype(BF16)
    x2d = _ffn(x2d, ffn_norm[0], wgu_layers, wd_layers, 0)
    qkv, z, ab, abt = _inproj1(x2d, bsz, seq, odd_norm[0], odd_w_in[0], odd_conv_w[0])
    x2d = _gdn(x2d, bsz, seq, qkv, z, ab, abt, odd_a_log[0], odd_dt_bias[0], odd_o_gain[0], odd_w_out[0])
    x2d = _ffn(x2d, ffn_norm[1], wgu_layers, wd_layers, 1)
    return x2d.reshape(bsz, seq, D_MODEL)
```

```python
import jax
import jax.numpy as jnp
from jax import lax
from jax.experimental import pallas as pl
from jax.experimental.pallas import tpu as pltpu

D_MODEL = 1024
D_FF = 2816
EPS = 1e-6

HEAD_DIM = 64
ATTN_HEADS = 8
ATTN_KV_HEADS = 2
ATTN_BLOCK = 128
ROPE_THETA = 10000.0
CONV_CH = 512
Q_W = ATTN_HEADS * HEAD_DIM
KV_W = ATTN_KV_HEADS * HEAD_DIM
DN_HEAD_DIM = 128
DN_HEADS = 8
DN_W = DN_HEADS * DN_HEAD_DIM
DN_CONV_WIDTH = 4
DN_CHUNK = 64

LANES = 128
SUBLANES = 8
MXU_DIM = 256
VMEM_LIMIT_BYTES = 56 * 1024 * 1024

ROW_TILE = 1024
FFN_ROWS = 1024
INPROJ1_ROWS = 1024
FFN_SPLITS = (0, 1536, D_FF)
MIX_ROWS = 1024
NEG_BIG = -1e30

BF16 = jnp.bfloat16
F32 = jnp.float32


def _rms_norm_rows(xf, gain):
    ms = jnp.mean(xf * xf, axis=-1, keepdims=True)
    return xf * lax.rsqrt(ms + EPS) * gain


def _sigmoid(v):
    return 1.0 / (1.0 + jnp.exp(-v))


def _silu(v):
    h = 0.5 * v
    return h + h * jnp.tanh(h)


def _dot(a, b):
    return jnp.dot(a, b, preferred_element_type=F32)


def _dot_nt(a, b):
    return lax.dot_general(a, b, (((1,), (1,)), ((), ())), preferred_element_type=F32)


def _dot_tn(a, b):
    return lax.dot_general(a, b, (((0,), (0,)), ((), ())), preferred_element_type=F32)


def _const_spec(shape):
    nd = len(shape)
    return pl.BlockSpec(shape, lambda *_: (0,) * nd, pipeline_mode=pl.Buffered(1))


def _block_ones(n, block):
    r = lax.broadcasted_iota(jnp.int32, (n, n), 0) // block
    c = lax.broadcasted_iota(jnp.int32, (n, n), 1) // block
    return (r == c).astype(BF16)


def _group_sumsq(v, block, scale):
    ones = _block_ones(MXU_DIM, block) * scale
    sq = (v * v).astype(BF16)
    parts = [_dot(sq[:, c:c + MXU_DIM], ones) for c in range(0, v.shape[1], MXU_DIM)]
    return parts[0] if len(parts) == 1 else jnp.concatenate(parts, axis=1)


def _causal_conv(cur, prev, w):
    width = w.shape[0]
    sub = lax.broadcasted_iota(jnp.int32, prev.shape, 0)

    def shift(a, a_prev, s):
        rolled = pltpu.roll(a, s, axis=0)
        head = jnp.where(sub < s, pltpu.roll(a_prev, s, axis=0), rolled[0:SUBLANES])
        return jnp.concatenate([head, rolled[SUBLANES:]], axis=0)

    def tap(j):
        return w[j:j + 1, :]

    even = tap(width - 1) * cur
    odd = tap(width - 2) * cur if width > 1 else None
    odd_prev = tap(width - 2) * prev if width > 1 else None
    for m in range(1, (width + 1) // 2):
        delayed = shift(cur, prev, 2 * m)
        even = even + tap(width - 1 - 2 * m) * delayed
        if width - 2 - 2 * m >= 0:
            odd = odd + tap(width - 2 - 2 * m) * delayed
            odd_prev = odd_prev + tap(width - 2 - 2 * m) * pltpu.roll(prev, 2 * m, axis=0)
    return even if odd is None else even + shift(odd, odd_prev, 1)


def _params(sem):
    return pltpu.CompilerParams(dimension_semantics=sem, vmem_limit_bytes=VMEM_LIMIT_BYTES)


def _ffn_rows(xf, gain_ref, wgu_ref, wd_ref):
    xn = _rms_norm_rows(xf, gain_ref[...]).astype(BF16)
    acc = xf
    for lo, hi in zip(FFN_SPLITS[:-1], FFN_SPLITS[1:]):
        g = _dot(xn, wgu_ref[:, lo:hi])
        u = _dot(xn, wgu_ref[:, D_FF + lo:D_FF + hi])
        act = (_silu(g) * u).astype(BF16)
        acc = acc + _dot(act, wd_ref[lo:hi, :])
    return acc


def _ffn_kernel(x_ref, gain_ref, wgu_ref, wd_ref, o_ref):
    o_ref[...] = _ffn_rows(x_ref[...], gain_ref, wgu_ref, wd_ref)


def _ffn(x2d, gain, wgu_layers, wd_layers, layer):
    n_rows = x2d.shape[0]
    row = lambda i: (i, 0)
    this_layer = lambda i: (layer, 0, 0)
    return pl.pallas_call(
        _ffn_kernel,
        grid=(n_rows // FFN_ROWS,),
        in_specs=[
            pl.BlockSpec((FFN_ROWS, D_MODEL), row),
            _const_spec((1, D_MODEL)),
            pl.BlockSpec((None, D_MODEL, 2 * D_FF), this_layer, pipeline_mode=pl.Buffered(1)),
            pl.BlockSpec((None, D_FF, D_MODEL), this_layer, pipeline_mode=pl.Buffered(1)),
        ],
        out_specs=pl.BlockSpec((FFN_ROWS, D_MODEL), row),
        out_shape=jax.ShapeDtypeStruct((n_rows, D_MODEL), F32),
        compiler_params=_params(("parallel",)),
        name="ffn",
    )(x2d, gain.reshape(1, D_MODEL), wgu_layers, wd_layers)


QK_COLS = Q_W + 2 * KV_W


def _inproj0_kernel(x_ref, gain_ref, w_ref, qkgain_ref, cos_ref, sin_ref,
                    q_ref, k_ref, v_ref, gb_ref, cu_ref):
    xn = _rms_norm_rows(x_ref[...], gain_ref[...]).astype(BF16)
    qk = _dot(xn, w_ref[:, :QK_COLS])
    c0 = QK_COLS
    v_ref[...] = _dot(xn, w_ref[:, c0:c0 + 2 * KV_W]).astype(BF16)
    c0 += 2 * KV_W
    ms = _group_sumsq(qk, HEAD_DIM, 1.0 / HEAD_DIM)
    gb_ref[...] = _dot(xn, w_ref[:, c0:c0 + CONV_CH]).astype(BF16)
    c0 += CONV_CH
    qk = qk * lax.rsqrt(ms + EPS) * qkgain_ref[...]
    gc = _dot(xn, w_ref[:, c0:c0 + CONV_CH])
    xin = _dot(xn, w_ref[:, c0 + CONV_CH:c0 + 2 * CONV_CH])
    cu_ref[...] = (gc * xin).astype(BF16)
    reps = QK_COLS // LANES
    cos = jnp.concatenate([cos_ref[...]] * reps, axis=1)
    sin = jnp.concatenate([sin_ref[...]] * reps, axis=1)
    half = HEAD_DIM // 2
    lane = lax.broadcasted_iota(jnp.int32, qk.shape, 1)
    partner = jnp.where((lane % HEAD_DIM) < half,
                        pltpu.roll(qk, QK_COLS - half, axis=1),
                        pltpu.roll(qk, half, axis=1))
    qk = qk * cos + partner * sin
    q_ref[...] = qk[:, :Q_W].astype(BF16)
    k_ref[...] = qk[:, Q_W:].astype(BF16)


def _dup_heads(w, n_heads, dim):
    w3 = w.reshape(w.shape[0], n_heads, 1, dim)
    return jnp.concatenate([w3, w3], axis=2).reshape(w.shape[0], 2 * n_heads * dim)


def _inproj0(x2d, seq, gain, w_in, q_gain, k_gain):
    n_rows = x2d.shape[0]
    wq = w_in[:, :Q_W]
    wk = _dup_heads(w_in[:, Q_W:Q_W + KV_W], ATTN_KV_HEADS, HEAD_DIM)
    wv = _dup_heads(w_in[:, Q_W + KV_W:Q_W + 2 * KV_W], ATTN_KV_HEADS, HEAD_DIM)
    w = jnp.concatenate([wq, wk, wv, w_in[:, Q_W + 2 * KV_W:]], axis=1).astype(BF16)
    n_cols = w.shape[1]
    qkgain = jnp.concatenate([jnp.tile(q_gain * (HEAD_DIM ** -0.5), ATTN_HEADS),
                              jnp.tile(k_gain, 2 * ATTN_KV_HEADS)]).reshape(1, QK_COLS).astype(F32)
    inv_freq = ROPE_THETA ** (-jnp.arange(0, HEAD_DIM, 2, dtype=F32) / HEAD_DIM)
    ang = jnp.arange(seq, dtype=F32)[:, None] * inv_freq[None, :]
    cos, sin = jnp.cos(ang), jnp.sin(ang)
    cos_t = jnp.tile(jnp.concatenate([cos, cos], axis=1), (1, LANES // HEAD_DIM))
    sin_t = jnp.tile(jnp.concatenate([-sin, sin], axis=1), (1, LANES // HEAD_DIM))
    tiles_per_seq = seq // ROW_TILE
    row = lambda i: (i, 0)
    pos = lambda i: (i % tiles_per_seq, 0)
    return pl.pallas_call(
        _inproj0_kernel,
        grid=(n_rows // ROW_TILE,),
        in_specs=[
            pl.BlockSpec((ROW_TILE, D_MODEL), row),
            _const_spec((1, D_MODEL)),
            _const_spec((D_MODEL, n_cols)),
            _const_spec((1, QK_COLS)),
            pl.BlockSpec((ROW_TILE, LANES), pos),
            pl.BlockSpec((ROW_TILE, LANES), pos),
        ],
        out_specs=[
            pl.BlockSpec((ROW_TILE, Q_W), row),
            pl.BlockSpec((ROW_TILE, 2 * KV_W), row),
            pl.BlockSpec((ROW_TILE, 2 * KV_W), row),
            pl.BlockSpec((ROW_TILE, CONV_CH), row),
            pl.BlockSpec((ROW_TILE, CONV_CH), row),
        ],
        out_shape=[
            jax.ShapeDtypeStruct((n_rows, Q_W), BF16),
            jax.ShapeDtypeStruct((n_rows, 2 * KV_W), BF16),
            jax.ShapeDtypeStruct((n_rows, 2 * KV_W), BF16),
            jax.ShapeDtypeStruct((n_rows, CONV_CH), BF16),
            jax.ShapeDtypeStruct((n_rows, CONV_CH), BF16),
        ],
        compiler_params=_params(("parallel",)),
        name="inproj0",
    )(x2d, gain.reshape(1, D_MODEL), w, qkgain, cos_t, sin_t)


CONV0_WIDTH = 3


def _mixer0_kernel(sinks_ref, q_ref, k_ref, v_ref, kp_ref, vp_ref, gb_ref, cu_ref, cup_ref,
                   x_ref, convw_ref, wout_ref, o_ref, y_ref):
    first = pl.program_id(1) == 0
    n_blocks = MIX_ROWS // ATTN_BLOCK
    grp = ATTN_HEADS // ATTN_KV_HEADS
    r = lax.broadcasted_iota(jnp.int32, (ATTN_BLOCK, 2 * ATTN_BLOCK), 0)
    c = lax.broadcasted_iota(jnp.int32, (ATTN_BLOCK, 2 * ATTN_BLOCK), 1)
    band = (c - r >= 1) & (c - r <= ATTN_BLOCK)
    lane = lax.broadcasted_iota(jnp.int32, (ATTN_BLOCK, LANES), 1)
    low_half = lane < HEAD_DIM
    block_rows = [slice(j * ATTN_BLOCK, (j + 1) * ATTN_BLOCK) for j in range(n_blocks)]

    def keys_values(ref, prev_ref, j, kv):
        before = prev_ref[:, kv * LANES:(kv + 1) * LANES] if j == 0 else ref[block_rows[j - 1], kv * LANES:(kv + 1) * LANES]
        return jnp.concatenate([before, ref[block_rows[j], kv * LANES:(kv + 1) * LANES]], axis=0)

    def head_queries(j, head):
        q2 = q_ref[block_rows[j], (head // 2) * LANES:(head // 2 + 1) * LANES]
        keep = low_half if head % 2 == 0 else jnp.logical_not(low_half)
        return jnp.where(keep, q2, jnp.zeros_like(q2))

    first_valid = band & ((c >= ATTN_BLOCK) | jnp.logical_not(first))
    heads = range(ATTN_HEADS)
    for j in range(n_blocks):
        valid = first_valid if j == 0 else band
        kk = [keys_values(k_ref, kp_ref, j, kv) for kv in range(ATTN_KV_HEADS)]
        vv = [keys_values(v_ref, vp_ref, j, kv) for kv in range(ATTN_KV_HEADS)]
        def head_slice(stacked, h):
            g = h % grp
            return stacked[h // grp][g * ATTN_BLOCK:(g + 1) * ATTN_BLOCK, :]

        s_g = [_dot_nt(jnp.concatenate([head_queries(j, kv * grp + g) for g in range(grp)], axis=0), kk[kv])
               for kv in range(ATTN_KV_HEADS)]
        s = [jnp.where(valid, head_slice(s_g, h), NEG_BIG) for h in heads]
        m = [jnp.maximum(jnp.max(s[h], axis=-1, keepdims=True), sinks_ref[h]) for h in heads]
        p = [jnp.exp(s[h] - m[h]) for h in heads]
        denom = [jnp.sum(p[h], axis=-1, keepdims=True) + jnp.exp(sinks_ref[h] - m[h]) for h in heads]
        o_g = [_dot(jnp.concatenate([p[kv * grp + g].astype(BF16) for g in range(grp)], axis=0), vv[kv])
               for kv in range(ATTN_KV_HEADS)]
        o = [head_slice(o_g, h) * (1.0 / denom[h]) for h in heads]
        for pair in range(ATTN_HEADS // 2):
            y_ref[block_rows[j], pair * LANES:(pair + 1) * LANES] = jnp.where(
                low_half, o[2 * pair], o[2 * pair + 1]).astype(BF16)
    prev_rows = cup_ref[...].astype(F32)
    prev_rows = jnp.where(first, jnp.zeros_like(prev_rows), prev_rows)
    conv = _causal_conv(cu_ref[...].astype(F32), prev_rows, convw_ref[...])
    y_ref[:, Q_W:] = (gb_ref[...].astype(F32) * conv).astype(BF16)
    o_ref[...] = x_ref[...] + _dot(y_ref[...], wout_ref[...])


def _mixer0(x2d, bsz, seq, q, k, v, gb, cu, sinks, conv_w, w_out):
    n_rows = x2d.shape[0]
    tiles = seq // MIX_ROWS
    blk_per_tile = MIX_ROWS // ATTN_BLOCK
    sub_per_tile = MIX_ROWS // SUBLANES
    row = lambda b, i: (b * tiles + i, 0)
    prev_blk = lambda b, i: (jnp.maximum((b * tiles + i) * blk_per_tile - 1, 0), 0)
    prev_sub = lambda b, i: (jnp.maximum((b * tiles + i) * sub_per_tile - 1, 0), 0)
    return pl.pallas_call(
        _mixer0_kernel,
        grid=(bsz, tiles),
        in_specs=[
            pl.BlockSpec(memory_space=pltpu.SMEM),
            pl.BlockSpec((MIX_ROWS, Q_W), row),
            pl.BlockSpec((MIX_ROWS, 2 * KV_W), row),
            pl.BlockSpec((MIX_ROWS, 2 * KV_W), row),
            pl.BlockSpec((ATTN_BLOCK, 2 * KV_W), prev_blk),
            pl.BlockSpec((ATTN_BLOCK, 2 * KV_W), prev_blk),
            pl.BlockSpec((MIX_ROWS, CONV_CH), row),
            pl.BlockSpec((MIX_ROWS, CONV_CH), row),
            pl.BlockSpec((SUBLANES, CONV_CH), prev_sub),
            pl.BlockSpec((MIX_ROWS, D_MODEL), row),
            _const_spec((CONV0_WIDTH, CONV_CH)),
            _const_spec((D_MODEL, D_MODEL)),
        ],
        out_specs=pl.BlockSpec((MIX_ROWS, D_MODEL), row),
        out_shape=jax.ShapeDtypeStruct((n_rows, D_MODEL), F32),
        scratch_shapes=[
            pltpu.VMEM((MIX_ROWS, D_MODEL), BF16),
        ],
        compiler_params=_params(("parallel", "arbitrary")),
        name="mixer0",
    )(sinks.astype(F32), q, k, v, k, v, gb, cu, cu, x2d, conv_w.astype(F32), w_out.astype(BF16))


GATE_LANES = LANES


def _inproj1_kernel(x_ref, gain_ref, w_ref, wab_ref, convw_ref, qkv_ref, z_ref, ab_ref, abt_ref, halo_ref):
    @pl.when(pl.program_id(1) == 0)
    def _():
        halo_ref[...] = jnp.zeros_like(halo_ref)

    xn = _rms_norm_rows(x_ref[...], gain_ref[...]).astype(BF16)
    ones = _block_ones(MXU_DIM, DN_HEAD_DIM)
    half_w = 0.5 * convw_ref[...]

    def project(c):
        return _dot(xn, w_ref[:, c * DN_W:(c + 1) * DN_W])

    def finish(c, proj):
        cols = slice(c * DN_W, (c + 1) * DN_W)
        h = _causal_conv(proj, halo_ref[:, cols], half_w[:, cols])
        act = h + h * jnp.tanh(h)
        halo_ref[:, cols] = proj[proj.shape[0] - SUBLANES:, :]
        if c < 2:
            scale = float(DN_HEAD_DIM) if c == 0 else 1.0
            summer = ones * scale if c == 0 else ones
            sq = (act * act).astype(BF16)
            ss = jnp.concatenate([_dot(sq[:, c0:c0 + MXU_DIM], summer) for c0 in range(0, DN_W, MXU_DIM)],
                                 axis=1)
            act = act * lax.rsqrt(ss + EPS * scale)
        qkv_ref[:, cols] = act.astype(BF16)

    proj_q = project(0)
    proj_k = project(1)
    finish(0, proj_q)
    proj_v = project(2)
    finish(1, proj_k)
    z_ref[...] = project(3).astype(BF16)
    finish(2, proj_v)
    ab = _dot(xn, wab_ref[...])
    ab_ref[...] = ab
    abt_ref[...] = ab.T[:2 * DN_HEADS, :]


def _inproj1(x2d, bsz, seq, gain, w_in, conv_w):
    n_rows = x2d.shape[0]
    w = w_in.astype(BF16)
    wab = jnp.pad(w_in[:, 4 * DN_W:], ((0, 0), (0, GATE_LANES - 2 * DN_HEADS))).astype(BF16)
    rows = INPROJ1_ROWS
    tiles = seq // rows
    row = lambda b, i: (b * tiles + i, 0)
    return pl.pallas_call(
        _inproj1_kernel,
        grid=(bsz, tiles),
        in_specs=[
            pl.BlockSpec((rows, D_MODEL), row),
            _const_spec((1, D_MODEL)),
            _const_spec((D_MODEL, 4 * DN_W)),
            _const_spec((D_MODEL, GATE_LANES)),
            _const_spec((DN_CONV_WIDTH, 3 * DN_W)),
        ],
        out_specs=[
            pl.BlockSpec((rows, 3 * DN_W), row),
            pl.BlockSpec((rows, DN_W), row),
            pl.BlockSpec((rows, GATE_LANES), row),
            pl.BlockSpec((2 * DN_HEADS, rows), lambda b, i: (0, b * tiles + i)),
        ],
        out_shape=[
            jax.ShapeDtypeStruct((n_rows, 3 * DN_W), BF16),
            jax.ShapeDtypeStruct((n_rows, DN_W), BF16),
            jax.ShapeDtypeStruct((n_rows, GATE_LANES), F32),
            jax.ShapeDtypeStruct((2 * DN_HEADS, n_rows), F32),
        ],
        scratch_shapes=[
            pltpu.VMEM((SUBLANES, 3 * DN_W), F32),
        ],
        compiler_params=_params(("parallel", "arbitrary")),
        name="inproj1",
    )(x2d, gain.reshape(1, D_MODEL), w, wab, conv_w.astype(F32))


GDN_ROWS = 1024
GDN_CHUNKS = GDN_ROWS // DN_CHUNK
GDN_GROUP = 8
SPLIT_TERMS = 3


def _split_bf16(v, terms):
    parts, rem = [], v
    for _ in range(terms):
        p = rem.astype(BF16)
        parts.append(p)
        rem = rem - p.astype(F32)
    return parts


def _softplus(v):
    return jnp.maximum(v, 0.0) + jnp.log(1.0 + jnp.exp(-jnp.abs(v)))


def _gdn_kernel(qkv_ref, z_ref, ab_ref, abt_ref, x_ref, gcol_ref, grow_ref, ogain_ref,
                wout_ref, o_ref, state_ref, gate_ref, o_acc_ref):
    @pl.when(pl.program_id(1) == 0)
    def _():
        state_ref[...] = jnp.zeros_like(state_ref)

    gcoef = gcol_ref[...]
    ab = ab_ref[...]
    gate_ref[:, 0:LANES] = gcoef[0:1, :] * _softplus(ab + gcoef[1:2, :])
    gate_ref[:, LANES:2 * LANES] = _sigmoid(ab)
    rcoef = grow_ref[...]

    ci = lax.broadcasted_iota(jnp.int32, (DN_CHUNK, LANES), 0)
    lane = lax.broadcasted_iota(jnp.int32, (DN_CHUNK, LANES), 1)
    cj = lane % DN_CHUNK
    left = lane < DN_CHUNK
    left_row = left[0:1, :]
    lower = ci >= cj
    strict = ci > cj
    eye = jnp.where(ci == cj, 1.0, 0.0)
    tril = lower[:, :DN_CHUNK].astype(BF16)
    triu2 = (ci <= cj).astype(BF16)
    heads = range(DN_HEADS)
    pairs = range(DN_HEADS // 2)
    zero_k = jnp.zeros((DN_CHUNK, LANES), BF16)
    zero_uw = jnp.zeros((DN_CHUNK, 2 * LANES), BF16)
    zero_s = jnp.zeros((DN_HEAD_DIM, DN_HEAD_DIM), BF16)

    def pair_diag(mat):
        return jnp.concatenate([jnp.where(left, mat, jnp.zeros_like(mat)),
                                jnp.where(left, jnp.zeros_like(mat), mat)], axis=0)

    def group_body(it, carry):
        chunks = [it * GDN_GROUP + j for j in range(GDN_GROUP)]
        rows = [pl.ds(c * DN_CHUNK, DN_CHUNK) for c in chunks]
        items = [(j, h) for j in range(GDN_GROUP) for h in heads]
        gc_col = [sum(_dot(tril, p) for p in _split_bf16(gate_ref[r, 0:LANES], SPLIT_TERMS))
                  for r in rows]
        g_row = [rcoef[:, 0:1] * _softplus(abt_ref[:, r] + rcoef[:, 1:2]) for r in rows]
        gc_row = [sum(_dot(p, triu2) for p in _split_bf16(g, SPLIT_TERMS)) for g in g_row]
        beta_c = [gate_ref[r, LANES:2 * LANES] for r in rows]

        def head_cols(j, h, group):
            return qkv_ref[rows[j], group * DN_W + h * LANES:group * DN_W + (h + 1) * LANES]

        q16 = [head_cols(j, h, 0) for j, h in items]
        k16 = [head_cols(j, h, 1) for j, h in items]
        k = [t.astype(F32) for t in k16]
        v = [head_cols(j, h, 2).astype(F32) for j, h in items]
        beta = [beta_c[j][:, h:h + 1] for j, h in items]
        gcc = [gc_col[j][:, DN_HEADS + h:DN_HEADS + h + 1] for j, h in items]
        gc_last = [g[DN_CHUNK - 1:DN_CHUNK, :] for g in gcc]
        n = range(len(items))
        eg = [jnp.exp(g) for g in gcc]
        kb = [k[i] * beta[i] for i in n]
        kb16 = [t.astype(BF16) for t in kb]
        pair_items = [(j, p) for j in range(GDN_GROUP) for p in pairs]
        m = range(len(pair_items))
        ia = [j * DN_HEADS + 2 * p for j, p in pair_items]
        ib = [i + 1 for i in ia]
        gcc2 = [jnp.where(left, gcc[ia[t]], gcc[ib[t]]) for t in m]
        gcr2 = [jnp.where(left_row, gc_row[j][DN_HEADS + 2 * p:DN_HEADS + 2 * p + 1, :],
                          gc_row[j][DN_HEADS + 2 * p + 1:DN_HEADS + 2 * p + 2, :]) for j, p in pair_items]
        gamma = [jnp.exp(jnp.where(lower, gcc2[t] - gcr2[t], -jnp.inf)) for t in m]
        s1 = [_dot_nt(jnp.concatenate([jnp.concatenate([kb16[ia[t]], kb16[ib[t]]], axis=1),
                                       jnp.concatenate([q16[ia[t]], q16[ib[t]]], axis=1)], axis=0),
                      jnp.concatenate([jnp.concatenate([k16[ia[t]], zero_k], axis=1),
                                       jnp.concatenate([zero_k, k16[ib[t]]], axis=1)], axis=0))
              for t in m]
        attn = [(s1[t][DN_CHUNK:] * gamma[t]).astype(BF16) for t in m]
        b = [jnp.where(strict, -(s1[t][:DN_CHUNK] * gamma[t]), 0.0).astype(BF16) for t in m]
        p = [eye + b[t].astype(F32) for t in m]
        b = [_dot(b[t], pair_diag(b[t])).astype(BF16) for t in m]
        power = 2
        while power < DN_CHUNK:
            last = 2 * power >= DN_CHUNK
            lhs = [p[t].astype(BF16) if last else jnp.concatenate([b[t], p[t].astype(BF16)], axis=0)
                   for t in m]
            prod = [_dot(lhs[t], pair_diag(b[t])) for t in m]
            if last:
                p = [p[t] + prod[t] for t in m]
            else:
                b = [prod[t][:DN_CHUNK].astype(BF16) for t in m]
                p = [p[t] + prod[t][DN_CHUNK:] for t in m]
            power *= 2
        t16 = [p[t].astype(BF16) for t in m]
        pair_of = {i: (t, 0) for t, i in enumerate(ia)}
        pair_of.update({i: (t, 1) for t, i in enumerate(ib)})

        def own_rows(i, mat, zero):
            return jnp.concatenate([mat, zero] if pair_of[i][1] == 0 else [zero, mat], axis=0)

        rhs = [jnp.concatenate([v[i] * beta[i], kb[i] * eg[i]], axis=1).astype(BF16) for i in n]
        uw = [_dot(t16[pair_of[i][0]], own_rows(i, rhs[i], zero_uw)).astype(BF16) for i in n]
        kd = [(k[i] * jnp.exp(gc_last[i] - gcc[i])).astype(BF16) for i in n]
        nm = [_dot_tn(kd[i], uw[i]) for i in n]
        oa = [_dot(attn[pair_of[i][0]], own_rows(i, uw[i], zero_uw)) for i in n]
        lhs = [jnp.concatenate([nm[i][:, LANES:], q16[i].astype(F32) * eg[i] - oa[i][:, LANES:]],
                               axis=0).astype(BF16) for i in n]
        decay = [jnp.exp(g) for g in gc_last]
        state = [state_ref[h] for h in heads]
        for j in range(GDN_GROUP):
            new_state = []
            for p_ in pairs:
                a, b_ = 2 * p_, 2 * p_ + 1
                i_a, i_b = j * DN_HEADS + a, j * DN_HEADS + b_
                s_diag = jnp.concatenate(
                    [jnp.concatenate([state[a].astype(BF16), zero_s], axis=1),
                     jnp.concatenate([zero_s, state[b_].astype(BF16)], axis=1)], axis=0)
                res = _dot(jnp.concatenate([lhs[i_a], lhs[i_b]], axis=1), s_diag)
                for h, i, cols in ((a, i_a, slice(0, LANES)), (b_, i_b, slice(LANES, 2 * LANES))):
                    new_state.append(decay[i] * state[h] - res[:DN_HEAD_DIM, cols] + nm[i][:, :LANES])
                    o_acc_ref[rows[j], h * LANES:(h + 1) * LANES] = res[DN_HEAD_DIM:, cols] + oa[i][:, :LANES]
            state = new_state
        for h in heads:
            state_ref[h] = state[h]
        return carry

    for it in range(GDN_CHUNKS // GDN_GROUP):
        group_body(it, 0)

    o = o_acc_ref[...]
    ms = _group_sumsq(o, DN_HEAD_DIM, 1.0 / DN_HEAD_DIM)
    y = o * lax.rsqrt(ms + EPS) * ogain_ref[...] * _silu(z_ref[...].astype(F32))
    o_ref[...] = x_ref[...] + _dot(y.astype(BF16), wout_ref[...])


def _gdn(x2d, bsz, seq, qkv, z, ab, abt, a_log, dt_bias, o_gain, w_out):
    n_rows = x2d.shape[0]
    neg_a = -jnp.exp(a_log.astype(F32))
    pad_lo = jnp.zeros((DN_HEADS,), F32)
    gcol = jnp.zeros((2, LANES), F32)
    gcol = gcol.at[0, DN_HEADS:2 * DN_HEADS].set(neg_a).at[1, DN_HEADS:2 * DN_HEADS].set(dt_bias.astype(F32))
    grow = jnp.stack([jnp.concatenate([pad_lo, neg_a]), jnp.concatenate([pad_lo, dt_bias.astype(F32)])], axis=1)
    ogain = jnp.tile(o_gain.astype(F32), DN_HEADS).reshape(1, DN_W)
    tiles = seq // GDN_ROWS
    row = lambda b, i: (b * tiles + i, 0)
    return pl.pallas_call(
        _gdn_kernel,
        grid=(bsz, tiles),
        in_specs=[
            pl.BlockSpec((GDN_ROWS, 3 * DN_W), row),
            pl.BlockSpec((GDN_ROWS, DN_W), row),
            pl.BlockSpec((GDN_ROWS, GATE_LANES), row),
            pl.BlockSpec((2 * DN_HEADS, GDN_ROWS), lambda b, i: (0, b * tiles + i)),
            pl.BlockSpec((GDN_ROWS, D_MODEL), row),
            _const_spec((2, LANES)),
            _const_spec((2 * DN_HEADS, 2)),
            _const_spec((1, DN_W)),
            _const_spec((DN_W, D_MODEL)),
        ],
        out_specs=pl.BlockSpec((GDN_ROWS, D_MODEL), row),
        out_shape=jax.ShapeDtypeStruct((n_rows, D_MODEL), F32),
        scratch_shapes=[
            pltpu.VMEM((DN_HEADS, DN_HEAD_DIM, DN_HEAD_DIM), F32),
            pltpu.VMEM((GDN_ROWS, 2 * LANES), F32),
            pltpu.VMEM((GDN_ROWS, DN_W), F32),
        ],
        compiler_params=_params(("parallel", "arbitrary")),
        name="gdn",
    )(qkv, z, ab, abt, x2d, gcol, grow, ogain, w_out.astype(BF16))


def kernel(x, even_norm, even_w_in, even_q_gain, even_k_gain, even_sinks, even_conv_w, even_w_out, odd_norm, odd_w_in, odd_conv_w, odd_a_log, odd_dt_bias, odd_o_gain, odd_w_out, ffn_norm, ffn_w_gate_up, ffn_w_down):
    bsz, seq, _ = x.shape
    x2d = x.reshape(bsz * seq, D_MODEL)
    q, k, v, gb, cu = _inproj0(x2d, seq, even_norm[0], even_w_in[0], even_q_gain[0], even_k_gain[0])
    x2d = _mixer0(x2d, bsz, seq, q, k, v, gb, cu, even_sinks[0], even_conv_w[0], even_w_out[0])
    wgu_layers = ffn_w_gate_up.astype(BF16)
    wd_layers = ffn_w_down.astype(BF16)
    x2d = _ffn(x2d, ffn_norm[0], wgu_layers, wd_layers, 0)
    qkv, z, ab, abt = _inproj1(x2d, bsz, seq, odd_norm[0], odd_w_in[0], odd_conv_w[0])
    x2d = _gdn(x2d, bsz, seq, qkv, z, ab, abt, odd_a_log[0], odd_dt_bias[0], odd_o_gain[0], odd_w_out[0])
    x2d = _ffn(x2d, ffn_norm[1], wgu_layers, wd_layers, 1)
    return x2d.reshape(bsz, seq, D_MODEL)
```

```python
import jax
import jax.numpy as jnp
from jax import lax
from jax.experimental import pallas as pl
from jax.experimental.pallas import tpu as pltpu

D_MODEL = 1024
D_FF = 2816
EPS = 1e-6

HEAD_DIM = 64
ATTN_HEADS = 8
ATTN_KV_HEADS = 2
ATTN_BLOCK = 128
ROPE_THETA = 10000.0
CONV_CH = 512
Q_W = ATTN_HEADS * HEAD_DIM
KV_W = ATTN_KV_HEADS * HEAD_DIM
DN_HEAD_DIM = 128
DN_HEADS = 8
DN_W = DN_HEADS * DN_HEAD_DIM
DN_CONV_WIDTH = 4
DN_CHUNK = 64

LANES = 128
SUBLANES = 8
MXU_DIM = 256
VMEM_LIMIT_BYTES = 56 * 1024 * 1024

FFN_ROWS = 1024
INPROJ1_ROWS = 1024
FFN_SPLITS = (0, 1536, D_FF)
MIX_ROWS = 1024
NEG_BIG = -1e30

BF16 = jnp.bfloat16
F32 = jnp.float32


def _rms_norm_rows(xf, gain):
    ms = jnp.mean(xf * xf, axis=-1, keepdims=True)
    return xf * lax.rsqrt(ms + EPS) * gain


def _sigmoid(v):
    return 1.0 / (1.0 + jnp.exp(-v))


def _silu(v):
    h = 0.5 * v
    return h + h * jnp.tanh(h)


def _dot(a, b):
    return jnp.dot(a, b, preferred_element_type=F32)


def _dot_nt(a, b):
    return lax.dot_general(a, b, (((1,), (1,)), ((), ())), preferred_element_type=F32)


def _dot_tn(a, b):
    return lax.dot_general(a, b, (((0,), (0,)), ((), ())), preferred_element_type=F32)


def _const_spec(shape):
    nd = len(shape)
    return pl.BlockSpec(shape, lambda *_: (0,) * nd, pipeline_mode=pl.Buffered(1))


def _block_ones(n, block):
    r = lax.broadcasted_iota(jnp.int32, (n, n), 0) // block
    c = lax.broadcasted_iota(jnp.int32, (n, n), 1) // block
    return (r == c).astype(BF16)


def _group_sumsq(v, block, scale):
    ones = _block_ones(MXU_DIM, block) * scale
    sq = (v * v).astype(BF16)
    parts = [_dot(sq[:, c:c + MXU_DIM], ones) for c in range(0, v.shape[1], MXU_DIM)]
    return parts[0] if len(parts) == 1 else jnp.concatenate(parts, axis=1)


def _causal_conv(cur, prev, w):
    width = w.shape[0]
    sub = lax.broadcasted_iota(jnp.int32, prev.shape, 0)

    def shift(a, a_prev, s):
        rolled = pltpu.roll(a, s, axis=0)
        head = jnp.where(sub < s, pltpu.roll(a_prev, s, axis=0), rolled[0:SUBLANES])
        return jnp.concatenate([head, rolled[SUBLANES:]], axis=0)

    def tap(j):
        return w[j:j + 1, :]

    even = tap(width - 1) * cur
    odd = tap(width - 2) * cur if width > 1 else None
    odd_prev = tap(width - 2) * prev if width > 1 else None
    for m in range(1, (width + 1) // 2):
        delayed = shift(cur, prev, 2 * m)
        even = even + tap(width - 1 - 2 * m) * delayed
        if width - 2 - 2 * m >= 0:
            odd = odd + tap(width - 2 - 2 * m) * delayed
            odd_prev = odd_prev + tap(width - 2 - 2 * m) * pltpu.roll(prev, 2 * m, axis=0)
    return even if odd is None else even + shift(odd, odd_prev, 1)


def _params(sem):
    return pltpu.CompilerParams(dimension_semantics=sem, vmem_limit_bytes=VMEM_LIMIT_BYTES)


def _ffn_rows(xf, gain_ref, wgu_ref, wd_ref):
    xn = _rms_norm_rows(xf, gain_ref[...]).astype(BF16)
    acc = xf
    for lo, hi in zip(FFN_SPLITS[:-1], FFN_SPLITS[1:]):
        g = _dot(xn, wgu_ref[:, lo:hi])
        u = _dot(xn, wgu_ref[:, D_FF + lo:D_FF + hi])
        act = (_silu(g) * u).astype(BF16)
        acc = acc + _dot(act, wd_ref[lo:hi, :])
    return acc


def _ffn_kernel(x_ref, gain_ref, wgu_ref, wd_ref, o_ref):
    o_ref[...] = _ffn_rows(x_ref[...], gain_ref, wgu_ref, wd_ref)


def _ffn(x2d, gain, wgu_layers, wd_layers, layer):
    n_rows = x2d.shape[0]
    row = lambda i: (i, 0)
    this_layer = lambda i: (layer, 0, 0)
    return pl.pallas_call(
        _ffn_kernel,
        grid=(n_rows // FFN_ROWS,),
        in_specs=[
            pl.BlockSpec((FFN_ROWS, D_MODEL), row),
            _const_spec((1, D_MODEL)),
            pl.BlockSpec((None, D_MODEL, 2 * D_FF), this_layer, pipeline_mode=pl.Buffered(1)),
            pl.BlockSpec((None, D_FF, D_MODEL), this_layer, pipeline_mode=pl.Buffered(1)),
        ],
        out_specs=pl.BlockSpec((FFN_ROWS, D_MODEL), row),
        out_shape=jax.ShapeDtypeStruct((n_rows, D_MODEL), F32),
        compiler_params=_params(("parallel",)),
        name="ffn",
    )(x2d, gain.reshape(1, D_MODEL), wgu_layers, wd_layers)


QK_COLS = Q_W + 2 * KV_W


def _inproj0_kernel(x_ref, gain_ref, w_ref, qkgain_ref, cos_ref, sin_ref,
                    q_ref, k_ref, v_ref, gb_ref, cu_ref):
    xn = _rms_norm_rows(x_ref[...], gain_ref[...]).astype(BF16)
    qk = _dot(xn, w_ref[:, :QK_COLS])
    c0 = QK_COLS
    v_ref[...] = _dot(xn, w_ref[:, c0:c0 + 2 * KV_W]).astype(BF16)
    c0 += 2 * KV_W
    ms = _group_sumsq(qk, HEAD_DIM, 1.0 / HEAD_DIM)
    gb_ref[...] = _dot(xn, w_ref[:, c0:c0 + CONV_CH]).astype(BF16)
    c0 += CONV_CH
    qk = qk * lax.rsqrt(ms + EPS) * qkgain_ref[...]
    gc = _dot(xn, w_ref[:, c0:c0 + CONV_CH])
    xin = _dot(xn, w_ref[:, c0 + CONV_CH:c0 + 2 * CONV_CH])
    cu_ref[...] = (gc * xin).astype(BF16)
    reps = QK_COLS // LANES
    cos = jnp.concatenate([cos_ref[...]] * reps, axis=1)
    sin = jnp.concatenate([sin_ref[...]] * reps, axis=1)
    half = HEAD_DIM // 2
    lane = lax.broadcasted_iota(jnp.int32, qk.shape, 1)
    partner = jnp.where((lane % HEAD_DIM) < half,
                        pltpu.roll(qk, QK_COLS - half, axis=1),
                        pltpu.roll(qk, half, axis=1))
    qk = qk * cos + partner * sin
    q_ref[...] = qk[:, :Q_W].astype(BF16)
    k_ref[...] = qk[:, Q_W:].astype(BF16)


def _dup_heads(w, n_heads, dim):
    w3 = w.reshape(w.shape[0], n_heads, 1, dim)
    return jnp.concatenate([w3, w3], axis=2).reshape(w.shape[0], 2 * n_heads * dim)


def _inproj0_operands(seq, w_in, q_gain, k_gain):
    wq = w_in[:, :Q_W]
    wk = _dup_heads(w_in[:, Q_W:Q_W + KV_W], ATTN_KV_HEADS, HEAD_DIM)
    wv = _dup_heads(w_in[:, Q_W + KV_W:Q_W + 2 * KV_W], ATTN_KV_HEADS, HEAD_DIM)
    w = jnp.concatenate([wq, wk, wv, w_in[:, Q_W + 2 * KV_W:]], axis=1).astype(BF16)
    qkgain = jnp.concatenate([jnp.tile(q_gain * (HEAD_DIM ** -0.5), ATTN_HEADS),
                              jnp.tile(k_gain, 2 * ATTN_KV_HEADS)]).reshape(1, QK_COLS).astype(F32)
    inv_freq = ROPE_THETA ** (-jnp.arange(0, HEAD_DIM, 2, dtype=F32) / HEAD_DIM)
    ang = jnp.arange(seq, dtype=F32)[:, None] * inv_freq[None, :]
    cos, sin = jnp.cos(ang), jnp.sin(ang)
    cos_t = jnp.tile(jnp.concatenate([cos, cos], axis=1), (1, LANES // HEAD_DIM))
    sin_t = jnp.tile(jnp.concatenate([-sin, sin], axis=1), (1, LANES // HEAD_DIM))
    return w, qkgain, cos_t, sin_t


CONV0_WIDTH = 3


def _mixer0_kernel(sinks_ref, q_ref, k_ref, v_ref, kp_ref, vp_ref, gb_ref, cu_ref, cup_ref,
                   x_ref, convw_ref, wout_ref, o_ref, y_ref):
    first = pl.program_id(1) == 0
    n_blocks = MIX_ROWS // ATTN_BLOCK
    grp = ATTN_HEADS // ATTN_KV_HEADS
    r = lax.broadcasted_iota(jnp.int32, (ATTN_BLOCK, 2 * ATTN_BLOCK), 0)
    c = lax.broadcasted_iota(jnp.int32, (ATTN_BLOCK, 2 * ATTN_BLOCK), 1)
    band = (c - r >= 1) & (c - r <= ATTN_BLOCK)
    lane = lax.broadcasted_iota(jnp.int32, (ATTN_BLOCK, LANES), 1)
    low_half = lane < HEAD_DIM
    block_rows = [slice(j * ATTN_BLOCK, (j + 1) * ATTN_BLOCK) for j in range(n_blocks)]

    def keys_values(ref, prev_ref, j, kv):
        before = prev_ref[:, kv * LANES:(kv + 1) * LANES] if j == 0 else ref[block_rows[j - 1], kv * LANES:(kv + 1) * LANES]
        return jnp.concatenate([before, ref[block_rows[j], kv * LANES:(kv + 1) * LANES]], axis=0)

    def head_queries(j, head):
        q2 = q_ref[block_rows[j], (head // 2) * LANES:(head // 2 + 1) * LANES]
        keep = low_half if head % 2 == 0 else jnp.logical_not(low_half)
        return jnp.where(keep, q2, jnp.zeros_like(q2))

    first_valid = band & ((c >= ATTN_BLOCK) | jnp.logical_not(first))
    heads = range(ATTN_HEADS)
    for j in range(n_blocks):
        valid = first_valid if j == 0 else band
        kk = [keys_values(k_ref, kp_ref, j, kv) for kv in range(ATTN_KV_HEADS)]
        vv = [keys_values(v_ref, vp_ref, j, kv) for kv in range(ATTN_KV_HEADS)]
        def head_slice(stacked, h):
            g = h % grp
            return stacked[h // grp][g * ATTN_BLOCK:(g + 1) * ATTN_BLOCK, :]

        s_g = [_dot_nt(jnp.concatenate([head_queries(j, kv * grp + g) for g in range(grp)], axis=0), kk[kv])
               for kv in range(ATTN_KV_HEADS)]
        s = [jnp.where(valid, head_slice(s_g, h), NEG_BIG) for h in heads]
        m = [jnp.maximum(jnp.max(s[h], axis=-1, keepdims=True), sinks_ref[h]) for h in heads]
        p = [jnp.exp(s[h] - m[h]) for h in heads]
        denom = [jnp.sum(p[h], axis=-1, keepdims=True) + jnp.exp(sinks_ref[h] - m[h]) for h in heads]
        o_g = [_dot(jnp.concatenate([p[kv * grp + g].astype(BF16) for g in range(grp)], axis=0), vv[kv])
               for kv in range(ATTN_KV_HEADS)]
        o = [head_slice(o_g, h) * (1.0 / denom[h]) for h in heads]
        for pair in range(ATTN_HEADS // 2):
            y_ref[block_rows[j], pair * LANES:(pair + 1) * LANES] = jnp.where(
                low_half, o[2 * pair], o[2 * pair + 1]).astype(BF16)
    prev_rows = cup_ref[...].astype(F32)
    prev_rows = jnp.where(first, jnp.zeros_like(prev_rows), prev_rows)
    conv = _causal_conv(cu_ref[...].astype(F32), prev_rows, convw_ref[...])
    y_ref[:, Q_W:] = (gb_ref[...].astype(F32) * conv).astype(BF16)
    o_ref[...] = x_ref[...] + _dot(y_ref[...], wout_ref[...])


def _layer0_kernel(sinks_ref, x_ref, gain_ref, w_ref, qkgain_ref, cos_ref, sin_ref, convw_ref, wout_ref,
                   o_ref, q_ref, k_ref, v_ref, gb_ref, cu_ref, kp_ref, vp_ref, cup_ref, y_ref):
    first = pl.program_id(1) == 0

    @pl.when(first)
    def _():
        kp_ref[...] = jnp.zeros_like(kp_ref)
        vp_ref[...] = jnp.zeros_like(vp_ref)
        cup_ref[...] = jnp.zeros_like(cup_ref)

    @pl.when(jnp.logical_not(first))
    def _():
        kp_ref[...] = k_ref[MIX_ROWS - ATTN_BLOCK:, :]
        vp_ref[...] = v_ref[MIX_ROWS - ATTN_BLOCK:, :]
        cup_ref[...] = cu_ref[MIX_ROWS - SUBLANES:, :]

    _inproj0_kernel(x_ref, gain_ref, w_ref, qkgain_ref, cos_ref, sin_ref, q_ref, k_ref, v_ref, gb_ref, cu_ref)
    _mixer0_kernel(sinks_ref, q_ref, k_ref, v_ref, kp_ref, vp_ref, gb_ref, cu_ref, cup_ref,
                   x_ref, convw_ref, wout_ref, o_ref, y_ref)


def _layer0(x2d, bsz, seq, gain, w_in, q_gain, k_gain, sinks, conv_w, w_out):
    n_rows = x2d.shape[0]
    w, qkgain, cos_t, sin_t = _inproj0_operands(seq, w_in, q_gain, k_gain)
    tiles = seq // MIX_ROWS
    row = lambda b, i: (b * tiles + i, 0)
    pos = lambda b, i: (i, 0)
    return pl.pallas_call(
        _layer0_kernel,
        grid=(bsz, tiles),
        in_specs=[
            pl.BlockSpec(memory_space=pltpu.SMEM),
            pl.BlockSpec((MIX_ROWS, D_MODEL), row),
            _const_spec((1, D_MODEL)),
            _const_spec((D_MODEL, w.shape[1])),
            _const_spec((1, QK_COLS)),
            pl.BlockSpec((MIX_ROWS, LANES), pos),
            pl.BlockSpec((MIX_ROWS, LANES), pos),
            _const_spec((CONV0_WIDTH, CONV_CH)),
            _const_spec((D_MODEL, D_MODEL)),
        ],
        out_specs=pl.BlockSpec((MIX_ROWS, D_MODEL), row),
        out_shape=jax.ShapeDtypeStruct((n_rows, D_MODEL), F32),
        scratch_shapes=[
            pltpu.VMEM((MIX_ROWS, Q_W), BF16),
            pltpu.VMEM((MIX_ROWS, 2 * KV_W), BF16),
            pltpu.VMEM((MIX_ROWS, 2 * KV_W), BF16),
            pltpu.VMEM((MIX_ROWS, CONV_CH), BF16),
            pltpu.VMEM((MIX_ROWS, CONV_CH), BF16),
            pltpu.VMEM((ATTN_BLOCK, 2 * KV_W), BF16),
            pltpu.VMEM((ATTN_BLOCK, 2 * KV_W), BF16),
            pltpu.VMEM((SUBLANES, CONV_CH), BF16),
            pltpu.VMEM((MIX_ROWS, D_MODEL), BF16),
        ],
        compiler_params=_params(("parallel", "arbitrary")),
        name="layer0",
    )(sinks.astype(F32), x2d, gain.reshape(1, D_MODEL), w, qkgain, cos_t, sin_t,
      conv_w.astype(F32), w_out.astype(BF16))


GATE_LANES = LANES


def _inproj1_kernel(x_ref, gain_ref, w_ref, wab_ref, convw_ref, qkv_ref, z_ref, ab_ref, abt_ref, halo_ref):
    @pl.when(pl.program_id(1) == 0)
    def _():
        halo_ref[...] = jnp.zeros_like(halo_ref)

    xn = _rms_norm_rows(x_ref[...], gain_ref[...]).astype(BF16)
    ones = _block_ones(MXU_DIM, DN_HEAD_DIM)
    half_w = 0.5 * convw_ref[...]

    def project(c):
        return _dot(xn, w_ref[:, c * DN_W:(c + 1) * DN_W])

    def finish(c, proj):
        cols = slice(c * DN_W, (c + 1) * DN_W)
        h = _causal_conv(proj, halo_ref[:, cols], half_w[:, cols])
        act = h + h * jnp.tanh(h)
        halo_ref[:, cols] = proj[proj.shape[0] - SUBLANES:, :]
        if c < 2:
            scale = float(DN_HEAD_DIM) if c == 0 else 1.0
            summer = ones * scale if c == 0 else ones
            sq = (act * act).astype(BF16)
            ss = jnp.concatenate([_dot(sq[:, c0:c0 + MXU_DIM], summer) for c0 in range(0, DN_W, MXU_DIM)],
                                 axis=1)
            act = act * lax.rsqrt(ss + EPS * scale)
        qkv_ref[:, cols] = act.astype(BF16)

    proj_q = project(0)
    proj_k = project(1)
    finish(0, proj_q)
    proj_v = project(2)
    finish(1, proj_k)
    z_ref[...] = project(3).astype(BF16)
    finish(2, proj_v)
    ab = _dot(xn, wab_ref[...])
    ab_ref[...] = ab
    abt_ref[...] = ab.T[:2 * DN_HEADS, :]


def _inproj1(x2d, bsz, seq, gain, w_in, conv_w):
    n_rows = x2d.shape[0]
    w = w_in.astype(BF16)
    wab = jnp.pad(w_in[:, 4 * DN_W:], ((0, 0), (0, GATE_LANES - 2 * DN_HEADS))).astype(BF16)
    rows = INPROJ1_ROWS
    tiles = seq // rows
    row = lambda b, i: (b * tiles + i, 0)
    return pl.pallas_call(
        _inproj1_kernel,
        grid=(bsz, tiles),
        in_specs=[
            pl.BlockSpec((rows, D_MODEL), row),
            _const_spec((1, D_MODEL)),
            _const_spec((D_MODEL, 4 * DN_W)),
            _const_spec((D_MODEL, GATE_LANES)),
            _const_spec((DN_CONV_WIDTH, 3 * DN_W)),
        ],
        out_specs=[
            pl.BlockSpec((rows, 3 * DN_W), row),
            pl.BlockSpec((rows, DN_W), row),
            pl.BlockSpec((rows, GATE_LANES), row),
            pl.BlockSpec((2 * DN_HEADS, rows), lambda b, i: (0, b * tiles + i)),
        ],
        out_shape=[
            jax.ShapeDtypeStruct((n_rows, 3 * DN_W), BF16),
            jax.ShapeDtypeStruct((n_rows, DN_W), BF16),
            jax.ShapeDtypeStruct((n_rows, GATE_LANES), F32),
            jax.ShapeDtypeStruct((2 * DN_HEADS, n_rows), F32),
        ],
        scratch_shapes=[
            pltpu.VMEM((SUBLANES, 3 * DN_W), F32),
        ],
        compiler_params=_params(("parallel", "arbitrary")),
        name="inproj1",
    )(x2d, gain.reshape(1, D_MODEL), w, wab, conv_w.astype(F32))


GDN_ROWS = 1024
GDN_CHUNKS = GDN_ROWS // DN_CHUNK
GDN_GROUP = 8
SPLIT_TERMS = 3


def _split_bf16(v, terms):
    parts, rem = [], v
    for _ in range(terms):
        p = rem.astype(BF16)
        parts.append(p)
        rem = rem - p.astype(F32)
    return parts


def _softplus(v):
    return jnp.maximum(v, 0.0) + jnp.log(1.0 + jnp.exp(-jnp.abs(v)))


def _gdn_kernel(qkv_ref, z_ref, ab_ref, abt_ref, x_ref, gcol_ref, grow_ref, ogain_ref,
                wout_ref, o_ref, state_ref, gate_ref, o_acc_ref):
    @pl.when(pl.program_id(1) == 0)
    def _():
        state_ref[...] = jnp.zeros_like(state_ref)

    gcoef = gcol_ref[...]
    ab = ab_ref[...]
    gate_ref[:, 0:LANES] = gcoef[0:1, :] * _softplus(ab + gcoef[1:2, :])
    gate_ref[:, LANES:2 * LANES] = _sigmoid(ab)
    rcoef = grow_ref[...]

    ci = lax.broadcasted_iota(jnp.int32, (DN_CHUNK, LANES), 0)
    lane = lax.broadcasted_iota(jnp.int32, (DN_CHUNK, LANES), 1)
    cj = lane % DN_CHUNK
    left = lane < DN_CHUNK
    left_row = left[0:1, :]
    lower = ci >= cj
    strict = ci > cj
    eye = jnp.where(ci == cj, 1.0, 0.0)
    tril = lower[:, :DN_CHUNK].astype(BF16)
    triu2 = (ci <= cj).astype(BF16)
    heads = range(DN_HEADS)
    pairs = range(DN_HEADS // 2)
    zero_k = jnp.zeros((DN_CHUNK, LANES), BF16)
    zero_uw = jnp.zeros((DN_CHUNK, 2 * LANES), BF16)
    zero_s = jnp.zeros((DN_HEAD_DIM, DN_HEAD_DIM), BF16)

    def pair_diag(mat):
        return jnp.concatenate([jnp.where(left, mat, jnp.zeros_like(mat)),
                                jnp.where(left, jnp.zeros_like(mat), mat)], axis=0)

    def group_body(it, carry):
        chunks = [it * GDN_GROUP + j for j in range(GDN_GROUP)]
        rows = [pl.ds(c * DN_CHUNK, DN_CHUNK) for c in chunks]
        items = [(j, h) for j in range(GDN_GROUP) for h in heads]
        gc_col = [sum(_dot(tril, p) for p in _split_bf16(gate_ref[r, 0:LANES], SPLIT_TERMS))
                  for r in rows]
        g_row = [rcoef[:, 0:1] * _softplus(abt_ref[:, r] + rcoef[:, 1:2]) for r in rows]
        gc_row = [sum(_dot(p, triu2) for p in _split_bf16(g, SPLIT_TERMS)) for g in g_row]
        beta_c = [gate_ref[r, LANES:2 * LANES] for r in rows]

        def head_cols(j, h, group):
            return qkv_ref[rows[j], group * DN_W + h * LANES:group * DN_W + (h + 1) * LANES]

        q16 = [head_cols(j, h, 0) for j, h in items]
        k16 = [head_cols(j, h, 1) for j, h in items]
        k = [t.astype(F32) for t in k16]
        v = [head_cols(j, h, 2).astype(F32) for j, h in items]
        beta = [beta_c[j][:, h:h + 1] for j, h in items]
        gcc = [gc_col[j][:, DN_HEADS + h:DN_HEADS + h + 1] for j, h in items]
        gc_last = [g[DN_CHUNK - 1:DN_CHUNK, :] for g in gcc]
        n = range(len(items))
        eg = [jnp.exp(g) for g in gcc]
        kb = [k[i] * beta[i] for i in n]
        kb16 = [t.astype(BF16) for t in kb]
        pair_items = [(j, p) for j in range(GDN_GROUP) for p in pairs]
        m = range(len(pair_items))
        ia = [j * DN_HEADS + 2 * p for j, p in pair_items]
        ib = [i + 1 for i in ia]
        gcc2 = [jnp.where(left, gcc[ia[t]], gcc[ib[t]]) for t in m]
        gcr2 = [jnp.where(left_row, gc_row[j][DN_HEADS + 2 * p:DN_HEADS + 2 * p + 1, :],
                          gc_row[j][DN_HEADS + 2 * p + 1:DN_HEADS + 2 * p + 2, :]) for j, p in pair_items]
        gamma = [jnp.exp(jnp.where(lower, gcc2[t] - gcr2[t], -jnp.inf)) for t in m]
        s1 = [_dot_nt(jnp.concatenate([jnp.concatenate([kb16[ia[t]], kb16[ib[t]]], axis=1),
                                       jnp.concatenate([q16[ia[t]], q16[ib[t]]], axis=1)], axis=0),
                      jnp.concatenate([jnp.concatenate([k16[ia[t]], zero_k], axis=1),
                                       jnp.concatenate([zero_k, k16[ib[t]]], axis=1)], axis=0))
              for t in m]
        attn = [(s1[t][DN_CHUNK:] * gamma[t]).astype(BF16) for t in m]
        b = [jnp.where(strict, -(s1[t][:DN_CHUNK] * gamma[t]), 0.0).astype(BF16) for t in m]
        p = [eye + b[t].astype(F32) for t in m]
        b = [_dot(b[t], pair_diag(b[t])).astype(BF16) for t in m]
        power = 2
        while power < DN_CHUNK:
            last = 2 * power >= DN_CHUNK
            lhs = [p[t].astype(BF16) if last else jnp.concatenate([b[t], p[t].astype(BF16)], axis=0)
                   for t in m]
            prod = [_dot(lhs[t], pair_diag(b[t])) for t in m]
            if last:
                p = [p[t] + prod[t] for t in m]
            else:
                b = [prod[t][:DN_CHUNK].astype(BF16) for t in m]
                p = [p[t] + prod[t][DN_CHUNK:] for t in m]
            power *= 2
        t16 = [p[t].astype(BF16) for t in m]
        pair_of = {i: (t, 0) for t, i in enumerate(ia)}
        pair_of.update({i: (t, 1) for t, i in enumerate(ib)})

        def own_rows(i, mat, zero):
            return jnp.concatenate([mat, zero] if pair_of[i][1] == 0 else [zero, mat], axis=0)

        rhs = [jnp.concatenate([v[i] * beta[i], kb[i] * eg[i]], axis=1).astype(BF16) for i in n]
        uw = [_dot(t16[pair_of[i][0]], own_rows(i, rhs[i], zero_uw)).astype(BF16) for i in n]
        kd = [(k[i] * jnp.exp(gc_last[i] - gcc[i])).astype(BF16) for i in n]
        nm = [_dot_tn(kd[i], uw[i]) for i in n]
        oa = [_dot(attn[pair_of[i][0]], own_rows(i, uw[i], zero_uw)) for i in n]
        lhs = [jnp.concatenate([nm[i][:, LANES:], q16[i].astype(F32) * eg[i] - oa[i][:, LANES:]],
                               axis=0).astype(BF16) for i in n]
        decay = [jnp.exp(g) for g in gc_last]
        state = [state_ref[h] for h in heads]
        for j in range(GDN_GROUP):
            new_state = []
            for p_ in pairs:
                a, b_ = 2 * p_, 2 * p_ + 1
                i_a, i_b = j * DN_HEADS + a, j * DN_HEADS + b_
                s_diag = jnp.concatenate(
                    [jnp.concatenate([state[a].astype(BF16), zero_s], axis=1),
                     jnp.concatenate([zero_s, state[b_].astype(BF16)], axis=1)], axis=0)
                res = _dot(jnp.concatenate([lhs[i_a], lhs[i_b]], axis=1), s_diag)
                for h, i, cols in ((a, i_a, slice(0, LANES)), (b_, i_b, slice(LANES, 2 * LANES))):
                    new_state.append(decay[i] * state[h] - res[:DN_HEAD_DIM, cols] + nm[i][:, :LANES])
                    o_acc_ref[rows[j], h * LANES:(h + 1) * LANES] = res[DN_HEAD_DIM:, cols] + oa[i][:, :LANES]
            state = new_state
        for h in heads:
            state_ref[h] = state[h]
        return carry

    for it in range(GDN_CHUNKS // GDN_GROUP):
        group_body(it, 0)

    o = o_acc_ref[...]
    ms = _group_sumsq(o, DN_HEAD_DIM, 1.0 / DN_HEAD_DIM)
    y = o * lax.rsqrt(ms + EPS) * ogain_ref[...] * _silu(z_ref[...].astype(F32))
    o_ref[...] = x_ref[...] + _dot(y.astype(BF16), wout_ref[...])


def _gdn(x2d, bsz, seq, qkv, z, ab, abt, a_log, dt_bias, o_gain, w_out):
    n_rows = x2d.shape[0]
    neg_a = -jnp.exp(a_log.astype(F32))
    pad_lo = jnp.zeros((DN_HEADS,), F32)
    gcol = jnp.zeros((2, LANES), F32)
    gcol = gcol.at[0, DN_HEADS:2 * DN_HEADS].set(neg_a).at[1, DN_HEADS:2 * DN_HEADS].set(dt_bias.astype(F32))
    grow = jnp.stack([jnp.concatenate([pad_lo, neg_a]), jnp.concatenate([pad_lo, dt_bias.astype(F32)])], axis=1)
    ogain = jnp.tile(o_gain.astype(F32), DN_HEADS).reshape(1, DN_W)
    tiles = seq // GDN_ROWS
    row = lambda b, i: (b * tiles + i, 0)
    return pl.pallas_call(
        _gdn_kernel,
        grid=(bsz, tiles),
        in_specs=[
            pl.BlockSpec((GDN_ROWS, 3 * DN_W), row),
            pl.BlockSpec((GDN_ROWS, DN_W), row),
            pl.BlockSpec((GDN_ROWS, GATE_LANES), row),
            pl.BlockSpec((2 * DN_HEADS, GDN_ROWS), lambda b, i: (0, b * tiles + i)),
            pl.BlockSpec((GDN_ROWS, D_MODEL), row),
            _const_spec((2, LANES)),
            _const_spec((2 * DN_HEADS, 2)),
            _const_spec((1, DN_W)),
            _const_spec((DN_W, D_MODEL)),
        ],
        out_specs=pl.BlockSpec((GDN_ROWS, D_MODEL), row),
        out_shape=jax.ShapeDtypeStruct((n_rows, D_MODEL), F32),
        scratch_shapes=[
            pltpu.VMEM((DN_HEADS, DN_HEAD_DIM, DN_HEAD_DIM), F32),
            pltpu.VMEM((GDN_ROWS, 2 * LANES), F32),
            pltpu.VMEM((GDN_ROWS, DN_W), F32),
        ],
        compiler_params=_params(("parallel", "arbitrary")),
        name="gdn",
    )(qkv, z, ab, abt, x2d, gcol, grow, ogain, w_out.astype(BF16))


def kernel(x, even_norm, even_w_in, even_q_gain, even_k_gain, even_sinks, even_conv_w, even_w_out, odd_norm, odd_w_in, odd_conv_w, odd_a_log, odd_dt_bias, odd_o_gain, odd_w_out, ffn_norm, ffn_w_gate_up, ffn_w_down):
    bsz, seq, _ = x.shape
    x2d = x.reshape(bsz * seq, D_MODEL)
    x2d = _layer0(x2d, bsz, seq, even_norm[0], even_w_in[0], even_q_gain[0], even_k_gain[0],
                  even_sinks[0], even_conv_w[0], even_w_out[0])
    wgu_layers = ffn_w_gate_up.astype(BF16)
    wd_layers = ffn_w_down.astype(BF16)
    x2d = _ffn(x2d, ffn_norm[0], wgu_layers, wd_layers, 0)
    qkv, z, ab, abt = _inproj1(x2d, bsz, seq, odd_norm[0], odd_w_in[0], odd_conv_w[0])
    x2d = _gdn(x2d, bsz, seq, qkv, z, ab, abt, odd_a_log[0], odd_dt_bias[0], odd_o_gain[0], odd_w_out[0])
    x2d = _ffn(x2d, ffn_norm[1], wgu_layers, wd_layers, 1)
    return x2d.reshape(bsz, seq, D_MODEL)
```

```python
import jax
import jax.numpy as jnp
from jax import lax
from jax.experimental import pallas as pl
from jax.experimental.pallas import tpu as pltpu

D_MODEL = 1024
D_FF = 2816
EPS = 1e-6

HEAD_DIM = 64
ATTN_HEADS = 8
ATTN_KV_HEADS = 2
ATTN_BLOCK = 128
ROPE_THETA = 10000.0
CONV_CH = 512
Q_W = ATTN_HEADS * HEAD_DIM
KV_W = ATTN_KV_HEADS * HEAD_DIM
DN_HEAD_DIM = 128
DN_HEADS = 8
DN_W = DN_HEADS * DN_HEAD_DIM
DN_CONV_WIDTH = 4
DN_CHUNK = 64

LANES = 128
SUBLANES = 8
MXU_DIM = 256
VMEM_LIMIT_BYTES = 56 * 1024 * 1024

ROW_TILE = 1024
FFN_ROWS = 1024
INPROJ1_ROWS = 1024
FFN_SPLITS = (0, 1536, D_FF)
MIX_ROWS = 1024
NEG_BIG = -1e30

BF16 = jnp.bfloat16
F32 = jnp.float32


def _rms_norm_rows(xf, gain):
    ms = jnp.mean(xf * xf, axis=-1, keepdims=True)
    return xf * lax.rsqrt(ms + EPS) * gain


def _sigmoid(v):
    return 1.0 / (1.0 + jnp.exp(-v))


def _silu(v):
    h = 0.5 * v
    return h + h * jnp.tanh(h)


def _dot(a, b):
    return jnp.dot(a, b, preferred_element_type=F32)


def _dot_nt(a, b):
    return lax.dot_general(a, b, (((1,), (1,)), ((), ())), preferred_element_type=F32)


def _dot_tn(a, b):
    return lax.dot_general(a, b, (((0,), (0,)), ((), ())), preferred_element_type=F32)


def _const_spec(shape):
    nd = len(shape)
    return pl.BlockSpec(shape, lambda *_: (0,) * nd, pipeline_mode=pl.Buffered(1))


def _block_ones(n, block):
    r = lax.broadcasted_iota(jnp.int32, (n, n), 0) // block
    c = lax.broadcasted_iota(jnp.int32, (n, n), 1) // block
    return (r == c).astype(BF16)


def _group_sumsq(v, block, scale):
    ones = _block_ones(MXU_DIM, block) * scale
    sq = (v * v).astype(BF16)
    parts = [_dot(sq[:, c:c + MXU_DIM], ones) for c in range(0, v.shape[1], MXU_DIM)]
    return parts[0] if len(parts) == 1 else jnp.concatenate(parts, axis=1)


def _causal_conv(cur, prev, w):
    width = w.shape[0]
    sub = lax.broadcasted_iota(jnp.int32, prev.shape, 0)

    def shift(a, a_prev, s):
        rolled = pltpu.roll(a, s, axis=0)
        head = jnp.where(sub < s, pltpu.roll(a_prev, s, axis=0), rolled[0:SUBLANES])
        return jnp.concatenate([head, rolled[SUBLANES:]], axis=0)

    def tap(j):
        return w[j:j + 1, :]

    even = tap(width - 1) * cur
    odd = tap(width - 2) * cur if width > 1 else None
    odd_prev = tap(width - 2) * prev if width > 1 else None
    for m in range(1, (width + 1) // 2):
        delayed = shift(cur, prev, 2 * m)
        even = even + tap(width - 1 - 2 * m) * delayed
        if width - 2 - 2 * m >= 0:
            odd = odd + tap(width - 2 - 2 * m) * delayed
            odd_prev = odd_prev + tap(width - 2 - 2 * m) * pltpu.roll(prev, 2 * m, axis=0)
    return even if odd is None else even + shift(odd, odd_prev, 1)


def _params(sem):
    return pltpu.CompilerParams(dimension_semantics=sem, vmem_limit_bytes=VMEM_LIMIT_BYTES)


def _ffn_rows(xf, gain_ref, wgu_ref, wd_ref):
    xn = _rms_norm_rows(xf, gain_ref[...]).astype(BF16)
    acc = xf
    for lo, hi in zip(FFN_SPLITS[:-1], FFN_SPLITS[1:]):
        g = _dot(xn, wgu_ref[:, lo:hi])
        u = _dot(xn, wgu_ref[:, D_FF + lo:D_FF + hi])
        act = (_silu(g) * u).astype(BF16)
        acc = acc + _dot(act, wd_ref[lo:hi, :])
    return acc


def _ffn_kernel(x_ref, gain_ref, wgu_ref, wd_ref, o_ref):
    o_ref[...] = _ffn_rows(x_ref[...], gain_ref, wgu_ref, wd_ref)


def _ffn(x2d, gain, wgu_layers, wd_layers, layer):
    n_rows = x2d.shape[0]
    row = lambda i: (i, 0)
    this_layer = lambda i: (layer, 0, 0)
    return pl.pallas_call(
        _ffn_kernel,
        grid=(n_rows // FFN_ROWS,),
        in_specs=[
            pl.BlockSpec((FFN_ROWS, D_MODEL), row),
            _const_spec((1, D_MODEL)),
            pl.BlockSpec((None, D_MODEL, 2 * D_FF), this_layer, pipeline_mode=pl.Buffered(1)),
            pl.BlockSpec((None, D_FF, D_MODEL), this_layer, pipeline_mode=pl.Buffered(1)),
        ],
        out_specs=pl.BlockSpec((FFN_ROWS, D_MODEL), row),
        out_shape=jax.ShapeDtypeStruct((n_rows, D_MODEL), F32),
        compiler_params=_params(("parallel",)),
        name="ffn",
    )(x2d, gain.reshape(1, D_MODEL), wgu_layers, wd_layers)


QK_COLS = Q_W + 2 * KV_W


def _inproj0_kernel(x_ref, gain_ref, w_ref, qkgain_ref, cos_ref, sin_ref,
                    q_ref, k_ref, v_ref, gb_ref, cu_ref):
    xn = _rms_norm_rows(x_ref[...], gain_ref[...]).astype(BF16)
    qkv = _dot(xn, w_ref[:, :Q_W + 2 * KV_W])
    lane2 = lax.broadcasted_iota(jnp.int32, (qkv.shape[0], LANES), 1)

    def doubled(pair):
        swapped = pltpu.roll(pair, HEAD_DIM, axis=1)
        low = lane2 < HEAD_DIM
        return jnp.concatenate([jnp.where(low, pair, swapped), jnp.where(low, swapped, pair)], axis=1)

    qk = jnp.concatenate([qkv[:, :Q_W], doubled(qkv[:, Q_W:Q_W + KV_W])], axis=1)
    v_ref[...] = doubled(qkv[:, Q_W + KV_W:]).astype(BF16)
    c0 = Q_W + 2 * KV_W
    ms = _group_sumsq(qk, HEAD_DIM, 1.0 / HEAD_DIM)
    gb_ref[...] = _dot(xn, w_ref[:, c0:c0 + CONV_CH]).astype(BF16)
    c0 += CONV_CH
    qk = qk * lax.rsqrt(ms + EPS) * qkgain_ref[...]
    gc = _dot(xn, w_ref[:, c0:c0 + CONV_CH])
    xin = _dot(xn, w_ref[:, c0 + CONV_CH:c0 + 2 * CONV_CH])
    cu_ref[...] = (gc * xin).astype(BF16)
    reps = QK_COLS // LANES
    cos = jnp.concatenate([cos_ref[...]] * reps, axis=1)
    sin = jnp.concatenate([sin_ref[...]] * reps, axis=1)
    half = HEAD_DIM // 2
    lane = lax.broadcasted_iota(jnp.int32, qk.shape, 1)
    partner = jnp.where((lane % HEAD_DIM) < half,
                        pltpu.roll(qk, QK_COLS - half, axis=1),
                        pltpu.roll(qk, half, axis=1))
    qk = qk * cos + partner * sin
    q_ref[...] = qk[:, :Q_W].astype(BF16)
    k_ref[...] = qk[:, Q_W:].astype(BF16)


def _inproj0(x2d, seq, gain, w_in, q_gain, k_gain):
    n_rows = x2d.shape[0]
    w = w_in.astype(BF16)
    n_cols = w.shape[1]
    qkgain = jnp.concatenate([jnp.tile(q_gain * (HEAD_DIM ** -0.5), ATTN_HEADS),
                              jnp.tile(k_gain, 2 * ATTN_KV_HEADS)]).reshape(1, QK_COLS).astype(F32)
    inv_freq = ROPE_THETA ** (-jnp.arange(0, HEAD_DIM, 2, dtype=F32) / HEAD_DIM)
    ang = jnp.arange(seq, dtype=F32)[:, None] * inv_freq[None, :]
    cos, sin = jnp.cos(ang), jnp.sin(ang)
    cos_t = jnp.tile(jnp.concatenate([cos, cos], axis=1), (1, LANES // HEAD_DIM))
    sin_t = jnp.tile(jnp.concatenate([-sin, sin], axis=1), (1, LANES // HEAD_DIM))
    tiles_per_seq = seq // ROW_TILE
    row = lambda i: (i, 0)
    pos = lambda i: (i % tiles_per_seq, 0)
    return pl.pallas_call(
        _inproj0_kernel,
        grid=(n_rows // ROW_TILE,),
        in_specs=[
            pl.BlockSpec((ROW_TILE, D_MODEL), row),
            _const_spec((1, D_MODEL)),
            _const_spec((D_MODEL, n_cols)),
            _const_spec((1, QK_COLS)),
            pl.BlockSpec((ROW_TILE, LANES), pos),
            pl.BlockSpec((ROW_TILE, LANES), pos),
        ],
        out_specs=[
            pl.BlockSpec((ROW_TILE, Q_W), row),
            pl.BlockSpec((ROW_TILE, 2 * KV_W), row),
            pl.BlockSpec((ROW_TILE, 2 * KV_W), row),
            pl.BlockSpec((ROW_TILE, CONV_CH), row),
            pl.BlockSpec((ROW_TILE, CONV_CH), row),
        ],
        out_shape=[
            jax.ShapeDtypeStruct((n_rows, Q_W), BF16),
            jax.ShapeDtypeStruct((n_rows, 2 * KV_W), BF16),
            jax.ShapeDtypeStruct((n_rows, 2 * KV_W), BF16),
            jax.ShapeDtypeStruct((n_rows, CONV_CH), BF16),
            jax.ShapeDtypeStruct((n_rows, CONV_CH), BF16),
        ],
        compiler_params=_params(("parallel",)),
        name="inproj0",
    )(x2d, gain.reshape(1, D_MODEL), w, qkgain, cos_t, sin_t)


CONV0_WIDTH = 3


def _mixer0_kernel(sinks_ref, q_ref, k_ref, v_ref, kp_ref, vp_ref, gb_ref, cu_ref, cup_ref,
                   x_ref, convw_ref, wout_ref, o_ref, y_ref):
    first = pl.program_id(1) == 0
    n_blocks = MIX_ROWS // ATTN_BLOCK
    grp = ATTN_HEADS // ATTN_KV_HEADS
    r = lax.broadcasted_iota(jnp.int32, (ATTN_BLOCK, 2 * ATTN_BLOCK), 0)
    c = lax.broadcasted_iota(jnp.int32, (ATTN_BLOCK, 2 * ATTN_BLOCK), 1)
    band = (c - r >= 1) & (c - r <= ATTN_BLOCK)
    lane = lax.broadcasted_iota(jnp.int32, (ATTN_BLOCK, LANES), 1)
    low_half = lane < HEAD_DIM
    block_rows = [slice(j * ATTN_BLOCK, (j + 1) * ATTN_BLOCK) for j in range(n_blocks)]

    def keys_values(ref, prev_ref, j, kv):
        before = prev_ref[:, kv * LANES:(kv + 1) * LANES] if j == 0 else ref[block_rows[j - 1], kv * LANES:(kv + 1) * LANES]
        return jnp.concatenate([before, ref[block_rows[j], kv * LANES:(kv + 1) * LANES]], axis=0)

    def head_queries(j, head):
        q2 = q_ref[block_rows[j], (head // 2) * LANES:(head // 2 + 1) * LANES]
        keep = low_half if head % 2 == 0 else jnp.logical_not(low_half)
        return jnp.where(keep, q2, jnp.zeros_like(q2))

    first_valid = band & ((c >= ATTN_BLOCK) | jnp.logical_not(first))
    heads = range(ATTN_HEADS)
    for j in range(n_blocks):
        valid = first_valid if j == 0 else band
        kk = [keys_values(k_ref, kp_ref, j, kv) for kv in range(ATTN_KV_HEADS)]
        vv = [keys_values(v_ref, vp_ref, j, kv) for kv in range(ATTN_KV_HEADS)]
        def head_slice(stacked, h):
            g = h % grp
            return stacked[h // grp][g * ATTN_BLOCK:(g + 1) * ATTN_BLOCK, :]

        s_g = [_dot_nt(jnp.concatenate([head_queries(j, kv * grp + g) for g in range(grp)], axis=0), kk[kv])
               for kv in range(ATTN_KV_HEADS)]
        s = [jnp.where(valid, head_slice(s_g, h), NEG_BIG) for h in heads]
        m = [jnp.maximum(jnp.max(s[h], axis=-1, keepdims=True), sinks_ref[h]) for h in heads]
        p = [jnp.exp(s[h] - m[h]) for h in heads]
        denom = [jnp.sum(p[h], axis=-1, keepdims=True) + jnp.exp(sinks_ref[h] - m[h]) for h in heads]
        o_g = [_dot(jnp.concatenate([p[kv * grp + g].astype(BF16) for g in range(grp)], axis=0), vv[kv])
               for kv in range(ATTN_KV_HEADS)]
        o = [head_slice(o_g, h) * (1.0 / denom[h]) for h in heads]
        for pair in range(ATTN_HEADS // 2):
            y_ref[block_rows[j], pair * LANES:(pair + 1) * LANES] = jnp.where(
                low_half, o[2 * pair], o[2 * pair + 1]).astype(BF16)
    prev_rows = cup_ref[...].astype(F32)
    prev_rows = jnp.where(first, jnp.zeros_like(prev_rows), prev_rows)
    conv = _causal_conv(cu_ref[...].astype(F32), prev_rows, convw_ref[...])
    y_ref[:, Q_W:] = (gb_ref[...].astype(F32) * conv).astype(BF16)
    o_ref[...] = x_ref[...] + _dot(y_ref[...], wout_ref[...])


def _mixer0(x2d, bsz, seq, q, k, v, gb, cu, sinks, conv_w, w_out):
    n_rows = x2d.shape[0]
    tiles = seq // MIX_ROWS
    blk_per_tile = MIX_ROWS // ATTN_BLOCK
    sub_per_tile = MIX_ROWS // SUBLANES
    row = lambda b, i: (b * tiles + i, 0)
    prev_blk = lambda b, i: (jnp.maximum((b * tiles + i) * blk_per_tile - 1, 0), 0)
    prev_sub = lambda b, i: (jnp.maximum((b * tiles + i) * sub_per_tile - 1, 0), 0)
    return pl.pallas_call(
        _mixer0_kernel,
        grid=(bsz, tiles),
        in_specs=[
            pl.BlockSpec(memory_space=pltpu.SMEM),
            pl.BlockSpec((MIX_ROWS, Q_W), row),
            pl.BlockSpec((MIX_ROWS, 2 * KV_W), row),
            pl.BlockSpec((MIX_ROWS, 2 * KV_W), row),
            pl.BlockSpec((ATTN_BLOCK, 2 * KV_W), prev_blk),
            pl.BlockSpec((ATTN_BLOCK, 2 * KV_W), prev_blk),
            pl.BlockSpec((MIX_ROWS, CONV_CH), row),
            pl.BlockSpec((MIX_ROWS, CONV_CH), row),
            pl.BlockSpec((SUBLANES, CONV_CH), prev_sub),
            pl.BlockSpec((MIX_ROWS, D_MODEL), row),
            _const_spec((CONV0_WIDTH, CONV_CH)),
            _const_spec((D_MODEL, D_MODEL)),
        ],
        out_specs=pl.BlockSpec((MIX_ROWS, D_MODEL), row),
        out_shape=jax.ShapeDtypeStruct((n_rows, D_MODEL), F32),
        scratch_shapes=[
            pltpu.VMEM((MIX_ROWS, D_MODEL), BF16),
        ],
        compiler_params=_params(("parallel", "arbitrary")),
        name="mixer0",
    )(sinks.astype(F32), q, k, v, k, v, gb, cu, cu, x2d, conv_w.astype(F32), w_out.astype(BF16))


GATE_LANES = LANES


def _inproj1_kernel(x_ref, gain_ref, w_ref, wab_ref, convw_ref, qkv_ref, z_ref, ab_ref, abt_ref, halo_ref):
    @pl.when(pl.program_id(1) == 0)
    def _():
        halo_ref[...] = jnp.zeros_like(halo_ref)

    xn = _rms_norm_rows(x_ref[...], gain_ref[...]).astype(BF16)
    ones = _block_ones(MXU_DIM, DN_HEAD_DIM)
    half_w = 0.5 * convw_ref[...]

    def project(c):
        return _dot(xn, w_ref[:, c * DN_W:(c + 1) * DN_W])

    def finish(c, proj):
        cols = slice(c * DN_W, (c + 1) * DN_W)
        h = _causal_conv(proj, halo_ref[:, cols], half_w[:, cols])
        act = h + h * jnp.tanh(h)
        halo_ref[:, cols] = proj[proj.shape[0] - SUBLANES:, :]
        if c < 2:
            scale = float(DN_HEAD_DIM) if c == 0 else 1.0
            summer = ones * scale if c == 0 else ones
            sq = (act * act).astype(BF16)
            ss = jnp.concatenate([_dot(sq[:, c0:c0 + MXU_DIM], summer) for c0 in range(0, DN_W, MXU_DIM)],
                                 axis=1)
            act = act * lax.rsqrt(ss + EPS * scale)
        qkv_ref[:, cols] = act.astype(BF16)

    proj_q = project(0)
    proj_k = project(1)
    finish(0, proj_q)
    proj_v = project(2)
    finish(1, proj_k)
    z_ref[...] = project(3).astype(BF16)
    finish(2, proj_v)
    ab = _dot(xn, wab_ref[...])
    ab_ref[...] = ab
    abt_ref[...] = ab.T[:2 * DN_HEADS, :]


def _inproj1(x2d, bsz, seq, gain, w_in, conv_w):
    n_rows = x2d.shape[0]
    w = w_in.astype(BF16)
    wab = jnp.pad(w_in[:, 4 * DN_W:], ((0, 0), (0, GATE_LANES - 2 * DN_HEADS))).astype(BF16)
    rows = INPROJ1_ROWS
    tiles = seq // rows
    row = lambda b, i: (b * tiles + i, 0)
    return pl.pallas_call(
        _inproj1_kernel,
        grid=(bsz, tiles),
        in_specs=[
            pl.BlockSpec((rows, D_MODEL), row),
            _const_spec((1, D_MODEL)),
            _const_spec((D_MODEL, 4 * DN_W)),
            _const_spec((D_MODEL, GATE_LANES)),
            _const_spec((DN_CONV_WIDTH, 3 * DN_W)),
        ],
        out_specs=[
            pl.BlockSpec((rows, 3 * DN_W), row),
            pl.BlockSpec((rows, DN_W), row),
            pl.BlockSpec((rows, GATE_LANES), row),
            pl.BlockSpec((2 * DN_HEADS, rows), lambda b, i: (0, b * tiles + i)),
        ],
        out_shape=[
            jax.ShapeDtypeStruct((n_rows, 3 * DN_W), BF16),
            jax.ShapeDtypeStruct((n_rows, DN_W), BF16),
            jax.ShapeDtypeStruct((n_rows, GATE_LANES), F32),
            jax.ShapeDtypeStruct((2 * DN_HEADS, n_rows), F32),
        ],
        scratch_shapes=[
            pltpu.VMEM((SUBLANES, 3 * DN_W), F32),
        ],
        compiler_params=_params(("parallel", "arbitrary")),
        name="inproj1",
    )(x2d, gain.reshape(1, D_MODEL), w, wab, conv_w.astype(F32))


GDN_ROWS = 1024
GDN_CHUNKS = GDN_ROWS // DN_CHUNK
GDN_GROUP = 8
SPLIT_TERMS = 3


def _split_bf16(v, terms):
    parts, rem = [], v
    for _ in range(terms):
        p = rem.astype(BF16)
        parts.append(p)
        rem = rem - p.astype(F32)
    return parts


def _softplus(v):
    return jnp.maximum(v, 0.0) + jnp.log(1.0 + jnp.exp(-jnp.abs(v)))


def _gdn_kernel(qkv_ref, z_ref, ab_ref, abt_ref, x_ref, gcol_ref, grow_ref, ogain_ref,
                wout_ref, o_ref, state_ref, gate_ref, o_acc_ref):
    @pl.when(pl.program_id(1) == 0)
    def _():
        state_ref[...] = jnp.zeros_like(state_ref)

    gcoef = gcol_ref[...]
    ab = ab_ref[...]
    gate_ref[:, 0:LANES] = gcoef[0:1, :] * _softplus(ab + gcoef[1:2, :])
    gate_ref[:, LANES:2 * LANES] = _sigmoid(ab)
    rcoef = grow_ref[...]

    ci = lax.broadcasted_iota(jnp.int32, (DN_CHUNK, LANES), 0)
    lane = lax.broadcasted_iota(jnp.int32, (DN_CHUNK, LANES), 1)
    cj = lane % DN_CHUNK
    left = lane < DN_CHUNK
    left_row = left[0:1, :]
    lower = ci >= cj
    strict = ci > cj
    eye = jnp.where(ci == cj, 1.0, 0.0)
    tril = lower[:, :DN_CHUNK].astype(BF16)
    triu2 = (ci <= cj).astype(BF16)
    heads = range(DN_HEADS)
    pairs = range(DN_HEADS // 2)
    zero_k = jnp.zeros((DN_CHUNK, LANES), BF16)
    zero_uw = jnp.zeros((DN_CHUNK, 2 * LANES), BF16)
    zero_s = jnp.zeros((DN_HEAD_DIM, DN_HEAD_DIM), BF16)

    def pair_diag(mat):
        return jnp.concatenate([jnp.where(left, mat, jnp.zeros_like(mat)),
                                jnp.where(left, jnp.zeros_like(mat), mat)], axis=0)

    def group_body(it, carry):
        chunks = [it * GDN_GROUP + j for j in range(GDN_GROUP)]
        rows = [pl.ds(c * DN_CHUNK, DN_CHUNK) for c in chunks]
        items = [(j, h) for j in range(GDN_GROUP) for h in heads]
        gc_col = [sum(_dot(tril, p) for p in _split_bf16(gate_ref[r, 0:LANES], SPLIT_TERMS))
                  for r in rows]
        g_row = [rcoef[:, 0:1] * _softplus(abt_ref[:, r] + rcoef[:, 1:2]) for r in rows]
        gc_row = [sum(_dot(p, triu2) for p in _split_bf16(g, SPLIT_TERMS)) for g in g_row]
        beta_c = [gate_ref[r, LANES:2 * LANES] for r in rows]

        def head_cols(j, h, group):
            return qkv_ref[rows[j], group * DN_W + h * LANES:group * DN_W + (h + 1) * LANES]

        q16 = [head_cols(j, h, 0) for j, h in items]
        k16 = [head_cols(j, h, 1) for j, h in items]
        k = [t.astype(F32) for t in k16]
        v = [head_cols(j, h, 2).astype(F32) for j, h in items]
        beta = [beta_c[j][:, h:h + 1] for j, h in items]
        gcc = [gc_col[j][:, DN_HEADS + h:DN_HEADS + h + 1] for j, h in items]
        gc_last = [g[DN_CHUNK - 1:DN_CHUNK, :] for g in gcc]
        n = range(len(items))
        eg = [jnp.exp(g) for g in gcc]
        kb = [k[i] * beta[i] for i in n]
        kb16 = [t.astype(BF16) for t in kb]
        pair_items = [(j, p) for j in range(GDN_GROUP) for p in pairs]
        m = range(len(pair_items))
        ia = [j * DN_HEADS + 2 * p for j, p in pair_items]
        ib = [i + 1 for i in ia]
        gcc2 = [jnp.where(left, gcc[ia[t]], gcc[ib[t]]) for t in m]
        gcr2 = [jnp.where(left_row, gc_row[j][DN_HEADS + 2 * p:DN_HEADS + 2 * p + 1, :],
                          gc_row[j][DN_HEADS + 2 * p + 1:DN_HEADS + 2 * p + 2, :]) for j, p in pair_items]
        gamma = [jnp.exp(jnp.where(lower, gcc2[t] - gcr2[t], -jnp.inf)) for t in m]
        s1 = [_dot_nt(jnp.concatenate([jnp.concatenate([kb16[ia[t]], kb16[ib[t]]], axis=1),
                                       jnp.concatenate([q16[ia[t]], q16[ib[t]]], axis=1)], axis=0),
                      jnp.concatenate([jnp.concatenate([k16[ia[t]], zero_k], axis=1),
                                       jnp.concatenate([zero_k, k16[ib[t]]], axis=1)], axis=0))
              for t in m]
        attn = [(s1[t][DN_CHUNK:] * gamma[t]).astype(BF16) for t in m]
        b = [jnp.where(strict, -(s1[t][:DN_CHUNK] * gamma[t]), 0.0).astype(BF16) for t in m]
        p = [eye + b[t].astype(F32) for t in m]
        b = [_dot(b[t], pair_diag(b[t])).astype(BF16) for t in m]
        power = 2
        while power < DN_CHUNK:
            last = 2 * power >= DN_CHUNK
            lhs = [p[t].astype(BF16) if last else jnp.concatenate([b[t], p[t].astype(BF16)], axis=0)
                   for t in m]
            prod = [_dot(lhs[t], pair_diag(b[t])) for t in m]
            if last:
                p = [p[t] + prod[t] for t in m]
            else:
                b = [prod[t][:DN_CHUNK].astype(BF16) for t in m]
                p = [p[t] + prod[t][DN_CHUNK:] for t in m]
            power *= 2
        t16 = [p[t].astype(BF16) for t in m]
        pair_of = {i: (t, 0) for t, i in enumerate(ia)}
        pair_of.update({i: (t, 1) for t, i in enumerate(ib)})

        def own_rows(i, mat, zero):
            return jnp.concatenate([mat, zero] if pair_of[i][1] == 0 else [zero, mat], axis=0)

        rhs = [jnp.concatenate([v[i] * beta[i], kb[i] * eg[i]], axis=1).astype(BF16) for i in n]
        uw = [_dot(t16[pair_of[i][0]], own_rows(i, rhs[i], zero_uw)).astype(BF16) for i in n]
        kd = [(k[i] * jnp.exp(gc_last[i] - gcc[i])).astype(BF16) for i in n]
        nm = [_dot_tn(kd[i], uw[i]) for i in n]
        oa = [_dot(attn[pair_of[i][0]], own_rows(i, uw[i], zero_uw)) for i in n]
        lhs = [jnp.concatenate([nm[i][:, LANES:], q16[i].astype(F32) * eg[i] - oa[i][:, LANES:]],
                               axis=0).astype(BF16) for i in n]
        decay = [jnp.exp(g) for g in gc_last]
        state = [state_ref[h] for h in heads]
        for j in range(GDN_GROUP):
            new_state = []
            for p_ in pairs:
                a, b_ = 2 * p_, 2 * p_ + 1
                i_a, i_b = j * DN_HEADS + a, j * DN_HEADS + b_
                s_diag = jnp.concatenate(
                    [jnp.concatenate([state[a].astype(BF16), zero_s], axis=1),
                     jnp.concatenate([zero_s, state[b_].astype(BF16)], axis=1)], axis=0)
                res = _dot(jnp.concatenate([lhs[i_a], lhs[i_b]], axis=1), s_diag)
                for h, i, cols in ((a, i_a, slice(0, LANES)), (b_, i_b, slice(LANES, 2 * LANES))):
                    new_state.append(decay[i] * state[h] - res[:DN_HEAD_DIM, cols] + nm[i][:, :LANES])
                    o_acc_ref[rows[j], h * LANES:(h + 1) * LANES] = res[DN_HEAD_DIM:, cols] + oa[i][:, :LANES]
            state = new_state
        for h in heads:
            state_ref[h] = state[h]
        return carry

    for it in range(GDN_CHUNKS // GDN_GROUP):
        group_body(it, 0)

    o = o_acc_ref[...]
    ms = _group_sumsq(o, DN_HEAD_DIM, 1.0 / DN_HEAD_DIM)
    y = o * lax.rsqrt(ms + EPS) * ogain_ref[...] * _silu(z_ref[...].astype(F32))
    o_ref[...] = x_ref[...] + _dot(y.astype(BF16), wout_ref[...])


def _gdn(x2d, bsz, seq, qkv, z, ab, abt, a_log, dt_bias, o_gain, w_out):
    n_rows = x2d.shape[0]
    neg_a = -jnp.exp(a_log.astype(F32))
    pad_lo = jnp.zeros((DN_HEADS,), F32)
    gcol = jnp.zeros((2, LANES), F32)
    gcol = gcol.at[0, DN_HEADS:2 * DN_HEADS].set(neg_a).at[1, DN_HEADS:2 * DN_HEADS].set(dt_bias.astype(F32))
    grow = jnp.stack([jnp.concatenate([pad_lo, neg_a]), jnp.concatenate([pad_lo, dt_bias.astype(F32)])], axis=1)
    ogain = jnp.tile(o_gain.astype(F32), DN_HEADS).reshape(1, DN_W)
    tiles = seq // GDN_ROWS
    row = lambda b, i: (b * tiles + i, 0)
    return pl.pallas_call(
        _gdn_kernel,
        grid=(bsz, tiles),
        in_specs=[
            pl.BlockSpec((GDN_ROWS, 3 * DN_W), row),
            pl.BlockSpec((GDN_ROWS, DN_W), row),
            pl.BlockSpec((GDN_ROWS, GATE_LANES), row),
            pl.BlockSpec((2 * DN_HEADS, GDN_ROWS), lambda b, i: (0, b * tiles + i)),
            pl.BlockSpec((GDN_ROWS, D_MODEL), row),
            _const_spec((2, LANES)),
            _const_spec((2 * DN_HEADS, 2)),
            _const_spec((1, DN_W)),
            _const_spec((DN_W, D_MODEL)),
        ],
        out_specs=pl.BlockSpec((GDN_ROWS, D_MODEL), row),
        out_shape=jax.ShapeDtypeStruct((n_rows, D_MODEL), F32),
        scratch_shapes=[
            pltpu.VMEM((DN_HEADS, DN_HEAD_DIM, DN_HEAD_DIM), F32),
            pltpu.VMEM((GDN_ROWS, 2 * LANES), F32),
            pltpu.VMEM((GDN_ROWS, DN_W), F32),
        ],
        compiler_params=_params(("parallel", "arbitrary")),
        name="gdn",
    )(qkv, z, ab, abt, x2d, gcol, grow, ogain, w_out.astype(BF16))


def kernel(x, even_norm, even_w_in, even_q_gain, even_k_gain, even_sinks, even_conv_w, even_w_out, odd_norm, odd_w_in, odd_conv_w, odd_a_log, odd_dt_bias, odd_o_gain, odd_w_out, ffn_norm, ffn_w_gate_up, ffn_w_down):
    bsz, seq, _ = x.shape
    x2d = x.reshape(bsz * seq, D_MODEL)
    q, k, v, gb, cu = _inproj0(x2d, seq, even_norm[0], even_w_in[0], even_q_gain[0], even_k_gain[0])
    x2d = _mixer0(x2d, bsz, seq, q, k, v, gb, cu, even_sinks[0], even_conv_w[0], even_w_out[0])
    wgu_layers = ffn_w_gate_up.astype(BF16)
    wd_layers = ffn_w_down.astype(BF16)
    x2d = _ffn(x2d, ffn_norm[0], wgu_layers, wd_layers, 0)
    qkv, z, ab, abt = _inproj1(x2d, bsz, seq, odd_norm[0], odd_w_in[0], odd_conv_w[0])
    x2d = _gdn(x2d, bsz, seq, qkv, z, ab, abt, odd_a_log[0], odd_dt_bias[0], odd_o_gain[0], odd_w_out[0])
    x2d = _ffn(x2d, ffn_norm[1], wgu_layers, wd_layers, 1)
    return x2d.reshape(bsz, seq, D_MODEL)
```
